```python
import jax, jax.numpy as jnp
from jax import lax
import numpy as np

D_MODEL = 1024
BATCH = 8
SEQ = 8192
DEPTH = 4
DEC_BATCH = 32
DEC_SEQ = 64
PAST_LEN = 1024

CHUNK = 64
D_CONV = D_MODEL
CONV_W = 3
RET_HEADS = 4
RET_DK = D_MODEL // 8
RET_DV = D_MODEL // 4
D_QK = RET_HEADS * RET_DK
D_RV = RET_HEADS * RET_DV
IN_SIZES = (D_CONV, D_CONV, D_CONV, D_CONV, D_QK, D_QK, D_RV, D_RV, D_MODEL, D_MODEL)
D_IN = sum(IN_SIZES)
ROPE_BASE = 10000.0
EPS = 1e-6

kernel_name = "gated_shortconv_retention_stream_step"


def rmsnorm(x, g):
    xf = x.astype(jnp.float32)
    y = xf * lax.rsqrt(jnp.mean(xf * xf, axis=-1, keepdims=True) + EPS)
    return (y * g.astype(jnp.float32)).astype(x.dtype)


def head_norm(o):
    return o * lax.rsqrt(jnp.mean(o * o, axis=-1, keepdims=True) + EPS)


def rotary(x, pos):
    half = RET_DK // 2
    inv = 1.0 / (ROPE_BASE ** (jnp.arange(half, dtype=jnp.float32) / half))
    ang = pos.astype(jnp.float32)[:, None] * inv[None, :]
    cos = jnp.cos(ang)[None, :, None, :]
    sin = jnp.sin(ang)[None, :, None, :]
    xf = x.astype(jnp.float32)
    x1, x2 = xf[..., :half], xf[..., half:]
    return jnp.concatenate([x1 * cos - x2 * sin, x1 * sin + x2 * cos], axis=-1)


def retention_chunk(q, k, v, s_prev, log_gamma):
    L = q.shape[2]
    idx = jnp.arange(L, dtype=jnp.float32)
    rel = idx[:, None] - idx[None, :]
    lg = log_gamma[:, None, None]
    decay = jnp.where(rel >= 0, jnp.exp(lg * jnp.maximum(rel, 0.0)), 0.0)
    scores = jnp.einsum('bhid,bhjd->bhij', q, k) * decay[None]
    o = jnp.einsum('bhij,bhjv->bhiv', scores, v)
    q_dec = jnp.exp(log_gamma[:, None] * (idx + 1.0))[None, :, :, None]
    o = o + jnp.einsum('bhid,bhdv->bhiv', q, s_prev) * q_dec
    k_dec = jnp.exp(log_gamma[:, None] * (L - 1.0 - idx))[None, :, :, None]
    s_new = (jnp.exp(log_gamma * L)[None, :, None, None] * s_prev
             + jnp.einsum('bhjd,bhjv->bhdv', k * k_dec, v))
    return o, s_new


def retention(q, k, v, s_prev, log_gamma):
    Bn, T = q.shape[0], q.shape[1]
    q, k, v = (a.transpose(0, 2, 1, 3) for a in (q, k, v))
    if T <= CHUNK:
        o, s = retention_chunk(q, k, v, s_prev, log_gamma)
    else:
        n = T // CHUNK

        def to_chunks(a):
            return a.reshape(Bn, RET_HEADS, n, CHUNK, a.shape[-1]).transpose(2, 0, 1, 3, 4)

        def body(s, qkv):
            qc, kc, vc = qkv
            oc, s_next = retention_chunk(qc, kc, vc, s, log_gamma)
            return s_next, oc

        s, oc = lax.scan(body, s_prev, (to_chunks(q), to_chunks(k), to_chunks(v)))
        o = oc.transpose(1, 2, 0, 3, 4).reshape(Bn, RET_HEADS, T, RET_DV)
    return o.transpose(0, 2, 1, 3), s


def mixer_layer(x, conv_buf, s_prev, pos, g, w_in, conv_w, w_a, w_r, w_o, log_gamma):
    Bn, T, _ = x.shape
    h = rmsnorm(x, g)
    p = h @ w_in
    split_at = np.cumsum(IN_SIZES)[:-1].tolist()
    hc, b, c, z_a, q, k, v, z_r, g_a, g_r = jnp.split(p, split_at, axis=-1)
    u = c * hc
    u_pad = jnp.concatenate([conv_buf.astype(u.dtype), u], axis=1)
    conv = (conv_w[0] * u_pad[:, 0:T] + conv_w[1] * u_pad[:, 1:T + 1]
            + conv_w[2] * u_pad[:, 2:T + 2])
    a = ((b * conv) * jax.nn.silu(z_a)) @ w_a
    new_buf = u_pad[:, -(CONV_W - 1):]
    qr = rotary(q.reshape(Bn, T, RET_HEADS, RET_DK), pos)
    kr = rotary(k.reshape(Bn, T, RET_HEADS, RET_DK), pos) * (RET_DK ** -0.5)
    vr = v.reshape(Bn, T, RET_HEADS, RET_DV).astype(jnp.float32)
    o, s_new = retention(qr, kr, vr, s_prev.astype(jnp.float32), log_gamma)
    o = head_norm(o).astype(x.dtype).reshape(Bn, T, D_RV)
    r = (o * jax.nn.silu(z_r)) @ w_r
    merged = jax.nn.sigmoid(g_a) * a + jax.nn.sigmoid(g_r) * r
    return x + merged @ w_o, new_buf, s_new


def setup_inputs(seed: int = 0) -> dict:
    key = jax.random.key(seed)
    ks = jax.random.split(key, 12)
    f32 = jnp.float32
    x_prompt = jax.random.normal(ks[0], (BATCH, SEQ, D_MODEL), f32)
    x_sample = jax.random.normal(ks[1], (DEC_BATCH, DEC_SEQ, D_MODEL), f32)
    state_conv = jax.random.normal(ks[2], (DEPTH, DEC_BATCH, CONV_W - 1, D_CONV), f32)
    state_ret = 0.5 * jax.random.normal(ks[3], (DEPTH, DEC_BATCH, RET_HEADS, RET_DK, RET_DV), f32)
    norm_g = 1.0 + 0.02 * jax.random.normal(ks[4], (DEPTH, D_MODEL), f32)
    w_in = jax.random.normal(ks[5], (DEPTH, D_MODEL, D_IN), f32) * D_MODEL ** -0.5
    conv_w = jax.random.normal(ks[6], (DEPTH, CONV_W, D_CONV), f32) * CONV_W ** -0.5
    w_a = jax.random.normal(ks[7], (DEPTH, D_CONV, D_MODEL), f32) * D_CONV ** -0.5
    w_r = jax.random.normal(ks[8], (DEPTH, D_RV, D_MODEL), f32) * D_RV ** -0.5
    w_o = jax.random.normal(ks[9], (DEPTH, D_MODEL, D_MODEL), f32) * D_MODEL ** -0.5
    final_g = 1.0 + 0.02 * jax.random.normal(ks[10], (D_MODEL,), f32)
    return {"x_prompt": x_prompt, "x_sample": x_sample, "state_conv": state_conv,
            "state_ret": state_ret, "norm_g": norm_g, "w_in": w_in, "conv_w": conv_w,
            "w_a": w_a, "w_r": w_r, "w_o": w_o, "final_g": final_g}


def reference(x_prompt, x_sample, state_conv, state_ret, norm_g, w_in, conv_w, w_a, w_r, w_o, final_g):
    log_gamma = jnp.log(1.0 - 2.0 ** (-5.0 - jnp.arange(RET_HEADS, dtype=jnp.float32)))
    Bp, Tp = x_prompt.shape[0], x_prompt.shape[1]
    Ts = x_sample.shape[1]
    pos_p = jnp.arange(Tp, dtype=jnp.int32)
    pos_s = PAST_LEN + jnp.arange(Ts, dtype=jnp.int32)
    buf_p0 = jnp.zeros((Bp, CONV_W - 1, D_CONV), x_prompt.dtype)
    s_p0 = jnp.zeros((Bp, RET_HEADS, RET_DK, RET_DV), jnp.float32)
    xp, xs = x_prompt, x_sample
    conv_p, ret_p, conv_s, ret_s = [], [], [], []
    for l in range(DEPTH):
        xp, bp, sp = mixer_layer(xp, buf_p0, s_p0, pos_p, norm_g[l], w_in[l], conv_w[l],
                                 w_a[l], w_r[l], w_o[l], log_gamma)
        xs, bs, ss = mixer_layer(xs, state_conv[l], state_ret[l], pos_s, norm_g[l], w_in[l],
                                 conv_w[l], w_a[l], w_r[l], w_o[l], log_gamma)
        conv_p.append(bp)
        ret_p.append(sp)
        conv_s.append(bs)
        ret_s.append(ss)
    y_prompt = rmsnorm(xp, final_g)
    y_sample = rmsnorm(xs, final_g)
    return (y_prompt, y_sample, jnp.stack(conv_p), jnp.stack(ret_p), jnp.stack(conv_s), jnp.stack(ret_s))
```

```python
import functools

import jax
import jax.numpy as jnp
from jax import lax
from jax.experimental import pallas as pl
from jax.experimental.pallas import tpu as pltpu

D_MODEL = 1024
DEPTH = 4
PAST_LEN = 1024
D_CONV = D_MODEL
CONV_W = 3
RET_HEADS = 4
RET_DK = D_MODEL // 8
RET_DV = D_MODEL // 4
D_QK = RET_HEADS * RET_DK
D_RV = RET_HEADS * RET_DV
ROPE_BASE = 10000.0
EPS = 1e-6

OFF_HC = 0
OFF_B = OFF_HC + D_CONV
OFF_C = OFF_B + D_CONV
OFF_ZA = OFF_C + D_CONV
OFF_Q = OFF_ZA + D_CONV
OFF_K = OFF_Q + D_QK
OFF_V = OFF_K + D_QK
OFF_ZR = OFF_V + D_RV
OFF_GA = OFF_ZR + D_RV
OFF_GR = OFF_GA + D_MODEL
D_IN = OFF_GR + D_MODEL

SUBLANES = 8
HALF = RET_DK // 2
MAX_CHUNK = 256
VMEM_LIMIT_BYTES = 60000 * 1024

F32 = jnp.float32
BF16 = jnp.bfloat16


def _rope_kernel(cos_ref, sin_ref, *, pos0, rows):
    shape = (rows, RET_DK)
    pos = lax.broadcasted_iota(jnp.int32, shape, 0) + (pl.program_id(0) * rows + pos0)
    lane = lax.broadcasted_iota(jnp.int32, shape, 1)
    idx = jnp.where(lane >= HALF, lane - HALF, lane).astype(F32)
    inv = 1.0 / (ROPE_BASE ** (idx / HALF))
    ang = pos.astype(F32) * inv
    cos_ref[...] = jnp.cos(ang)
    sin = jnp.sin(ang)
    sin_ref[...] = jnp.where(lane >= HALF, sin, -sin)


def _rope_tables(seq, pos0):
    rows = min(seq, 512)
    out = jax.ShapeDtypeStruct((seq, RET_DK), F32)
    spec = pl.BlockSpec((rows, RET_DK), lambda i: (i, 0))
    return pl.pallas_call(
        functools.partial(_rope_kernel, pos0=pos0, rows=rows),
        grid=(seq // rows,),
        out_specs=(spec, spec),
        out_shape=(out, out),
        name="rope_tables",
    )()


def _sigmoid(x):
    return 1.0 / (1.0 + jnp.exp(-x))


def _layer_kernel(lg_ref, x_ref, cos_ref, sin_ref, conv0_ref, ret0_ref, g_ref, cw_ref,
                  win_ref, wa_ref, wr_ref, wo_ref, *rest, nb, tt, chunk, final_norm):
    if final_norm:
        fg_ref, *rest = rest
    y_ref, convn_ref, retn_ref, s_ref, ubuf_ref, dec_ref, qdec_ref, kdec_ref = rest

    j = pl.program_id(1)
    nj = pl.num_programs(1)
    m = nb * tt

    @pl.when((pl.program_id(0) == 0) & (j == 0))
    def _init_decay_tables():
        row = lax.broadcasted_iota(jnp.int32, (chunk, chunk), 0)
        col = lax.broadcasted_iota(jnp.int32, (chunk, chunk), 1)
        rel = (row - col).astype(F32)
        rq = lax.broadcasted_iota(jnp.int32, (chunk, RET_DV), 0).astype(F32)
        rk = lax.broadcasted_iota(jnp.int32, (chunk, RET_DK), 0).astype(F32)
        for h in range(RET_HEADS):
            lg = lg_ref[h]
            dec_ref[h] = jnp.where(rel >= 0, jnp.exp(lg * jnp.maximum(rel, 0.0)), 0.0)
            qdec_ref[h] = jnp.exp(lg * (rq + 1.0))
            kdec_ref[h] = jnp.exp(lg * (chunk - 1.0 - rk))

    @pl.when(j == 0)
    def _load_state():
        s_ref[...] = ret0_ref[...]
        for s in range(nb):
            ubuf_ref[s, SUBLANES - 2:SUBLANES, :] = conv0_ref[s]

    xs = x_ref[...].reshape(m, D_MODEL)
    ms = jnp.mean(xs * xs, axis=-1, keepdims=True)
    hb = (xs * lax.rsqrt(ms + EPS) * g_ref[...]).astype(BF16)

    def proj(off, n):
        return jnp.dot(hb, win_ref[:, off:off + n], preferred_element_type=F32)

    u = proj(OFF_C, D_CONV) * proj(OFF_HC, D_CONV)
    convs = []
    for s in range(nb):
        us = u[s * tt:(s + 1) * tt]
        ubuf_ref[s, SUBLANES:SUBLANES + tt, :] = us
        u1 = ubuf_ref[s, SUBLANES - 1:SUBLANES - 1 + tt, :]
        u2 = ubuf_ref[s, SUBLANES - 2:SUBLANES - 2 + tt, :]
        convs.append(cw_ref[0:1, :] * u2 + cw_ref[1:2, :] * u1 + cw_ref[2:3, :] * us)
    conv = convs[0] if nb == 1 else jnp.concatenate(convs, axis=0)
    za = proj(OFF_ZA, D_CONV)
    a_in = ((proj(OFF_B, D_CONV) * conv) * (za * _sigmoid(za))).astype(BF16)
    a = jnp.dot(a_in, wa_ref[...], preferred_element_type=F32)
    merged = _sigmoid(proj(OFF_GA, D_MODEL)) * a

    for s in range(nb):
        tail = ubuf_ref[s, tt + SUBLANES - 2:tt + SUBLANES, :]
        ubuf_ref[s, SUBLANES - 2:SUBLANES, :] = tail

        @pl.when(j == nj - 1)
        def _store_conv_state():
            convn_ref[s] = tail

    q = proj(OFF_Q, D_QK)
    k = proj(OFF_K, D_QK)
    v = proj(OFF_V, D_RV).astype(BF16)
    zr = proj(OFF_ZR, D_RV)
    cos_t = cos_ref[...]
    sin_t = sin_ref[...]
    if nb > 1:
        cos_t = jnp.concatenate([cos_t] * nb, axis=0)
        sin_t = jnp.concatenate([sin_t] * nb, axis=0)

    def rotary(xh):
        return xh * cos_t + pltpu.roll(xh, HALF, 1) * sin_t

    o_heads = []
    for h in range(RET_HEADS):
        qh = rotary(q[:, h * RET_DK:(h + 1) * RET_DK])
        kh = rotary(k[:, h * RET_DK:(h + 1) * RET_DK]) * (RET_DK ** -0.5)
        vh = v[:, h * RET_DV:(h + 1) * RET_DV]
        g_chunk = jnp.exp(jnp.full((1, RET_DV), lg_ref[h] * chunk, F32))
        o_rows = []
        for s in range(nb):
            state = s_ref[s, h]
            for c in range(tt // chunk):
                r0 = s * tt + c * chunk
                qc = qh[r0:r0 + chunk].astype(BF16)
                kc = kh[r0:r0 + chunk]
                vc = vh[r0:r0 + chunk]
                scores = lax.dot_general(qc, kc.astype(BF16), (((1,), (1,)), ((), ())),
                                         preferred_element_type=F32) * dec_ref[h]
                o = jnp.dot(scores.astype(BF16), vc, preferred_element_type=F32)
                o = o + jnp.dot(qc, state.astype(BF16),
                                preferred_element_type=F32) * qdec_ref[h]
                kd = (kc * kdec_ref[h]).astype(BF16)
                state = g_chunk * state + lax.dot_general(
                    kd, vc, (((0,), (0,)), ((), ())), preferred_element_type=F32)
                o_rows.append(o * lax.rsqrt(jnp.mean(o * o, axis=-1, keepdims=True) + EPS))
            s_ref[s, h] = state
        o_heads.append(o_rows[0] if len(o_rows) == 1 else jnp.concatenate(o_rows, axis=0))
    o_all = jnp.concatenate(o_heads, axis=1)
    o_gated = (o_all * (zr * _sigmoid(zr))).astype(BF16)
    r = jnp.dot(o_gated, wr_ref[...], preferred_element_type=F32)
    merged = (merged + _sigmoid(proj(OFF_GR, D_MODEL)) * r).astype(BF16)

    out = xs + jnp.dot(merged, wo_ref[...], preferred_element_type=F32)
    if final_norm:
        ms_o = jnp.mean(out * out, axis=-1, keepdims=True)
        out = out * lax.rsqrt(ms_o + EPS) * fg_ref[...]
    y_ref[...] = out.reshape(nb, tt, D_MODEL)

    @pl.when(j == nj - 1)
    def _store_ret_state():
        retn_ref[...] = s_ref[...]


def _tile_plan(seq):
    tt = min(seq, 256)
    nb = max(1, 256 // tt)
    return nb, tt


def _mixer_layer(layer, state_layer, x, cos_t, sin_t, conv0, ret0, log_gamma, norm_g, conv_w,
                 win, wa, wr, wo, final_g):
    batch, seq, _ = x.shape
    nb, tt = _tile_plan(seq)
    chunk = min(tt, MAX_CHUNK)
    final_norm = final_g is not None
    once = pl.Buffered(1)

    def const_spec(shape):
        zeros = (0,) * len(shape)
        return pl.BlockSpec(shape, lambda b, j: zeros, pipeline_mode=once)

    def layer_spec(shape):
        zeros = (0,) * len(shape)
        return pl.BlockSpec((None,) + shape, lambda b, j: (layer,) + zeros, pipeline_mode=once)

    in_specs = [
        pl.BlockSpec(memory_space=pltpu.SMEM),
        pl.BlockSpec((nb, tt, D_MODEL), lambda b, j: (b, j, 0)),
        pl.BlockSpec((tt, RET_DK), lambda b, j: (j, 0)),
        pl.BlockSpec((tt, RET_DK), lambda b, j: (j, 0)),
        pl.BlockSpec((None, nb, CONV_W - 1, D_CONV), lambda b, j: (state_layer, b, 0, 0)),
        pl.BlockSpec((None, nb, RET_HEADS, RET_DK, RET_DV),
                     lambda b, j: (state_layer, b, 0, 0, 0)),
        layer_spec((1, D_MODEL)),
        layer_spec((CONV_W, D_CONV)),
        layer_spec((D_MODEL, D_IN)),
        layer_spec((D_CONV, D_MODEL)),
        layer_spec((D_RV, D_MODEL)),
        layer_spec((D_MODEL, D_MODEL)),
    ]
    args = [log_gamma, x, cos_t, sin_t, conv0, ret0, norm_g.reshape(DEPTH, 1, D_MODEL),
            conv_w, win, wa, wr, wo]
    if final_norm:
        in_specs.append(const_spec((1, D_MODEL)))
        args.append(final_g.reshape(1, D_MODEL))

    out_shape = (
        jax.ShapeDtypeStruct(x.shape, x.dtype),
        jax.ShapeDtypeStruct((batch, CONV_W - 1, D_CONV), x.dtype),
        jax.ShapeDtypeStruct((batch, RET_HEADS, RET_DK, RET_DV), F32),
    )
    out_specs = (
        pl.BlockSpec((nb, tt, D_MODEL), lambda b, j: (b, j, 0)),
        pl.BlockSpec((nb, CONV_W - 1, D_CONV), lambda b, j: (b, 0, 0)),
        pl.BlockSpec((nb, RET_HEADS, RET_DK, RET_DV), lambda b, j: (b, 0, 0, 0)),
    )
    scratch_shapes = [
        pltpu.VMEM((nb, RET_HEADS, RET_DK, RET_DV), F32),
        pltpu.VMEM((nb, tt + SUBLANES, D_CONV), F32),
        pltpu.VMEM((RET_HEADS, chunk, chunk), F32),
        pltpu.VMEM((RET_HEADS, chunk, RET_DV), F32),
        pltpu.VMEM((RET_HEADS, chunk, RET_DK), F32),
    ]
    return pl.pallas_call(
        functools.partial(_layer_kernel, nb=nb, tt=tt, chunk=chunk, final_norm=final_norm),
        grid=(batch // nb, seq // tt),
        in_specs=in_specs,
        out_specs=out_specs,
        out_shape=out_shape,
        scratch_shapes=scratch_shapes,
        compiler_params=pltpu.CompilerParams(
            dimension_semantics=("arbitrary", "arbitrary"),
            vmem_limit_bytes=VMEM_LIMIT_BYTES),
        name=f"mixer_layer_t{seq}",
    )(*args)


def kernel(x_prompt, x_sample, state_conv, state_ret, norm_g, w_in, conv_w, w_a, w_r, w_o, final_g):
    log_gamma = jnp.log(1.0 - 2.0 ** (-5.0 - jnp.arange(RET_HEADS, dtype=F32)))
    bp, tp, _ = x_prompt.shape
    ts = x_sample.shape[1]
    cos_p, sin_p = _rope_tables(tp, 0)
    cos_s, sin_s = _rope_tables(ts, PAST_LEN)
    win, wa, wr, wo = (w.astype(BF16) for w in (w_in, w_a, w_r, w_o))
    conv_p0 = jnp.zeros((1, bp, CONV_W - 1, D_CONV), x_prompt.dtype)
    ret_p0 = jnp.zeros((1, bp, RET_HEADS, RET_DK, RET_DV), F32)

    xp, xs = x_prompt, x_sample
    conv_p, ret_p, conv_s, ret_s = [], [], [], []
    for layer in range(DEPTH):
        fg = final_g if layer == DEPTH - 1 else None
        xp, cp, rp = _mixer_layer(layer, 0, xp, cos_p, sin_p, conv_p0, ret_p0, log_gamma,
                                  norm_g, conv_w, win, wa, wr, wo, fg)
        xs, cs, rs = _mixer_layer(layer, layer, xs, cos_s, sin_s, state_conv, state_ret,
                                  log_gamma, norm_g, conv_w, win, wa, wr, wo, fg)
        conv_p.append(cp)
        ret_p.append(rp)
        conv_s.append(cs)
        ret_s.append(rs)
    return (xp, xs, jnp.stack(conv_p), jnp.stack(ret_p), jnp.stack(conv_s), jnp.stack(ret_s))
```

```python
import functools

import jax
import jax.numpy as jnp
from jax import lax
from jax.experimental import pallas as pl
from jax.experimental.pallas import tpu as pltpu

D_MODEL = 1024
DEPTH = 4
PAST_LEN = 1024
D_CONV = D_MODEL
CONV_W = 3
RET_HEADS = 4
RET_DK = D_MODEL // 8
RET_DV = D_MODEL // 4
D_QK = RET_HEADS * RET_DK
D_RV = RET_HEADS * RET_DV
ROPE_BASE = 10000.0
EPS = 1e-6

OFF_HC = 0
OFF_B = OFF_HC + D_CONV
OFF_C = OFF_B + D_CONV
OFF_ZA = OFF_C + D_CONV
OFF_Q = OFF_ZA + D_CONV
OFF_K = OFF_Q + D_QK
OFF_V = OFF_K + D_QK
OFF_ZR = OFF_V + D_RV
OFF_GA = OFF_ZR + D_RV
OFF_GR = OFF_GA + D_MODEL
D_IN = OFF_GR + D_MODEL

SUBLANES = 8
HALF = RET_DK // 2
MAX_CHUNK = 256
TILE_ROWS = 512
VMEM_LIMIT_BYTES = 60000 * 1024

F32 = jnp.float32
BF16 = jnp.bfloat16


def _rope_kernel(cos_ref, sin_ref, *, pos0, rows):
    shape = (rows, RET_DK)
    pos = lax.broadcasted_iota(jnp.int32, shape, 0) + (pl.program_id(0) * rows + pos0)
    lane = lax.broadcasted_iota(jnp.int32, shape, 1)
    idx = jnp.where(lane >= HALF, lane - HALF, lane).astype(F32)
    inv = 1.0 / (ROPE_BASE ** (idx / HALF))
    ang = pos.astype(F32) * inv
    cos_ref[...] = jnp.cos(ang)
    sin = jnp.sin(ang)
    sin_ref[...] = jnp.where(lane >= HALF, sin, -sin)


def _rope_tables(seq, pos0):
    rows = min(seq, 512)
    out = jax.ShapeDtypeStruct((seq, RET_DK), F32)
    spec = pl.BlockSpec((rows, RET_DK), lambda i: (i, 0))
    return pl.pallas_call(
        functools.partial(_rope_kernel, pos0=pos0, rows=rows),
        grid=(seq // rows,),
        out_specs=(spec, spec),
        out_shape=(out, out),
        name="rope_tables",
    )()


def _twice_sigmoid_of_twice(half_x):
    return jnp.tanh(half_x) + 1.0


def _layer_kernel(lg_ref, x_ref, cos_ref, sin_ref, conv0_ref, ret0_ref, g_ref, cw_ref,
                  win_ref, wa_ref, wr_ref, wo_ref, *rest, nb, tt, chunk, final_norm):
    if final_norm:
        fg_ref, *rest = rest
    y_ref, convn_ref, retn_ref, s_ref, ubuf_ref, dec_ref, qdec_ref, kdec_ref = rest

    j = pl.program_id(1)
    nj = pl.num_programs(1)
    m = nb * tt

    @pl.when((pl.program_id(0) == 0) & (j == 0))
    def _init_decay_tables():
        row = lax.broadcasted_iota(jnp.int32, (chunk, chunk), 0)
        col = lax.broadcasted_iota(jnp.int32, (chunk, chunk), 1)
        rel = (row - col).astype(F32)
        rq = lax.broadcasted_iota(jnp.int32, (chunk, RET_DV), 0).astype(F32)
        rk = lax.broadcasted_iota(jnp.int32, (chunk, RET_DK), 0).astype(F32)
        for h in range(RET_HEADS):
            lg = lg_ref[h]
            dec_ref[h] = jnp.where(rel >= 0, jnp.exp(lg * jnp.maximum(rel, 0.0)), 0.0)
            qdec_ref[h] = jnp.exp(lg * (rq + 1.0))
            kdec_ref[h] = jnp.exp(lg * (chunk - 1.0 - rk))

    @pl.when(j == 0)
    def _load_state():
        s_ref[...] = ret0_ref[...]
        for s in range(nb):
            ubuf_ref[s, SUBLANES - 2:SUBLANES, :] = conv0_ref[s]

    xs = x_ref[...].reshape(m, D_MODEL)
    ms = jnp.mean(xs * xs, axis=-1, keepdims=True)
    hb = (xs * lax.rsqrt(ms + EPS) * g_ref[...]).astype(BF16)

    def proj(off, n):
        return jnp.dot(hb, win_ref[:, off:off + n], preferred_element_type=F32)

    u = proj(OFF_C, D_CONV) * proj(OFF_HC, D_CONV)
    convs = []
    for s in range(nb):
        us = u[s * tt:(s + 1) * tt]
        ubuf_ref[s, SUBLANES:SUBLANES + tt, :] = us
        u1 = ubuf_ref[s, SUBLANES - 1:SUBLANES - 1 + tt, :]
        u2 = ubuf_ref[s, SUBLANES - 2:SUBLANES - 2 + tt, :]
        convs.append(cw_ref[0:1, :] * u2 + cw_ref[1:2, :] * u1 + cw_ref[2:3, :] * us)
    conv = convs[0] if nb == 1 else jnp.concatenate(convs, axis=0)
    half_za = proj(OFF_ZA, D_CONV)
    silu_za = half_za * _twice_sigmoid_of_twice(half_za)
    a_in = ((proj(OFF_B, D_CONV) * conv) * silu_za).astype(BF16)
    gate_a = _twice_sigmoid_of_twice(proj(OFF_GA, D_MODEL))

    for s in range(nb):
        ubuf_ref[s, SUBLANES - 2:SUBLANES, :] = ubuf_ref[s, tt + SUBLANES - 2:tt + SUBLANES, :]

    q = proj(OFF_Q, D_QK)
    k = proj(OFF_K, D_QK)
    v = proj(OFF_V, D_RV).astype(BF16)
    merged = gate_a * jnp.dot(a_in, wa_ref[...], preferred_element_type=F32)
    cos_t = cos_ref[...]
    sin_t = sin_ref[...]
    if nb > 1:
        cos_t = jnp.concatenate([cos_t] * nb, axis=0)
        sin_t = jnp.concatenate([sin_t] * nb, axis=0)

    def rotary(xh):
        return xh * cos_t + pltpu.roll(xh, HALF, 1) * sin_t

    blocks = [(h, s, s * tt + c * chunk) for h in range(RET_HEADS) for s in range(nb)
              for c in range(tt // chunk)]
    qb, kf, scores = {}, {}, {}
    for h in range(RET_HEADS):
        qb[h] = rotary(q[:, h * RET_DK:(h + 1) * RET_DK]).astype(BF16)
        kf[h] = rotary(k[:, h * RET_DK:(h + 1) * RET_DK]) * (RET_DK ** -0.5)
    for h, s, r0 in blocks:
        sc = lax.dot_general(qb[h][r0:r0 + chunk], kf[h][r0:r0 + chunk].astype(BF16),
                             (((1,), (1,)), ((), ())), preferred_element_type=F32)
        scores[h, r0] = (sc * dec_ref[h]).astype(BF16)
    half_zr = proj(OFF_ZR, D_RV)
    silu_zr = half_zr * _twice_sigmoid_of_twice(half_zr)
    o_blocks, state = {}, {}
    for h, s, r0 in blocks:
        vc = v[r0:r0 + chunk, h * RET_DV:(h + 1) * RET_DV]
        st = state.get((h, s))
        if st is None:
            st = s_ref[s, h]
        o_cross = jnp.dot(qb[h][r0:r0 + chunk], st.astype(BF16),
                          preferred_element_type=F32) * qdec_ref[h]
        o = jnp.dot(scores[h, r0], vc, preferred_element_type=F32) + o_cross
        kd = (kf[h][r0:r0 + chunk] * kdec_ref[h]).astype(BF16)
        g_chunk = jnp.exp(jnp.full((1, RET_DV), lg_ref[h] * chunk, F32))
        state[h, s] = g_chunk * st + lax.dot_general(
            kd, vc, (((0,), (0,)), ((), ())), preferred_element_type=F32)
        o_blocks[h, r0] = o * lax.rsqrt(jnp.mean(o * o, axis=-1, keepdims=True) + EPS)
    for (h, s), st in state.items():
        s_ref[s, h] = st
    row_starts = sorted({r0 for _, _, r0 in blocks})
    o_all = jnp.concatenate(
        [jnp.concatenate([o_blocks[h, r0] for h in range(RET_HEADS)], axis=1)
         for r0 in row_starts], axis=0)
    o_gated = (o_all * silu_zr).astype(BF16)
    gate_r = _twice_sigmoid_of_twice(proj(OFF_GR, D_MODEL))
    half_r = jnp.dot(o_gated, wr_ref[...], preferred_element_type=F32)
    merged = (merged + gate_r * half_r).astype(BF16)

    out = xs + jnp.dot(merged, wo_ref[...], preferred_element_type=F32)
    if final_norm:
        ms_o = jnp.mean(out * out, axis=-1, keepdims=True)
        out = out * lax.rsqrt(ms_o + EPS) * fg_ref[...]
    y_ref[...] = out.reshape(nb, tt, D_MODEL)

    @pl.when(j == nj - 1)
    def _store_states():
        retn_ref[...] = s_ref[...]
        for s in range(nb):
            convn_ref[s] = ubuf_ref[s, SUBLANES - 2:SUBLANES, :]


def _tile_plan(seq):
    tt = min(seq, TILE_ROWS)
    nb = max(1, 256 // tt)
    return nb, tt


def _mixer_layer(layer, state_layer, x, cos_t, sin_t, conv0, ret0, log_gamma, norm_g, conv_w,
                 win, wa, wr, wo, final_g):
    batch, seq, _ = x.shape
    nb, tt = _tile_plan(seq)
    chunk = min(tt, MAX_CHUNK)
    final_norm = final_g is not None
    once = pl.Buffered(1)

    def const_spec(shape):
        zeros = (0,) * len(shape)
        return pl.BlockSpec(shape, lambda b, j: zeros, pipeline_mode=once)

    def layer_spec(shape):
        zeros = (0,) * len(shape)
        return pl.BlockSpec((None,) + shape, lambda b, j: (layer,) + zeros, pipeline_mode=once)

    in_specs = [
        pl.BlockSpec(memory_space=pltpu.SMEM),
        pl.BlockSpec((nb, tt, D_MODEL), lambda b, j: (b, j, 0)),
        pl.BlockSpec((tt, RET_DK), lambda b, j: (j, 0)),
        pl.BlockSpec((tt, RET_DK), lambda b, j: (j, 0)),
        pl.BlockSpec((None, nb, CONV_W - 1, D_CONV), lambda b, j: (state_layer, b, 0, 0)),
        pl.BlockSpec((None, nb, RET_HEADS, RET_DK, RET_DV),
                     lambda b, j: (state_layer, b, 0, 0, 0)),
        layer_spec((1, D_MODEL)),
        layer_spec((CONV_W, D_CONV)),
        layer_spec((D_MODEL, D_IN)),
        layer_spec((D_CONV, D_MODEL)),
        layer_spec((D_RV, D_MODEL)),
        layer_spec((D_MODEL, D_MODEL)),
    ]
    args = [log_gamma, x, cos_t, sin_t, conv0, ret0, norm_g.reshape(DEPTH, 1, D_MODEL),
            conv_w, win, wa, wr, wo]
    if final_norm:
        in_specs.append(const_spec((1, D_MODEL)))
        args.append(final_g.reshape(1, D_MODEL))

    out_shape = (
        jax.ShapeDtypeStruct(x.shape, x.dtype),
        jax.ShapeDtypeStruct((batch, CONV_W - 1, D_CONV), x.dtype),
        jax.ShapeDtypeStruct((batch, RET_HEADS, RET_DK, RET_DV), F32),
    )
    out_specs = (
        pl.BlockSpec((nb, tt, D_MODEL), lambda b, j: (b, j, 0)),
        pl.BlockSpec((nb, CONV_W - 1, D_CONV), lambda b, j: (b, 0, 0)),
        pl.BlockSpec((nb, RET_HEADS, RET_DK, RET_DV), lambda b, j: (b, 0, 0, 0)),
    )
    scratch_shapes = [
        pltpu.VMEM((nb, RET_HEADS, RET_DK, RET_DV), F32),
        pltpu.VMEM((nb, tt + SUBLANES, D_CONV), F32),
        pltpu.VMEM((RET_HEADS, chunk, chunk), F32),
        pltpu.VMEM((RET_HEADS, chunk, RET_DV), F32),
        pltpu.VMEM((RET_HEADS, chunk, RET_DK), F32),
    ]
    return pl.pallas_call(
        functools.partial(_layer_kernel, nb=nb, tt=tt, chunk=chunk, final_norm=final_norm),
        grid=(batch // nb, seq // tt),
        in_specs=in_specs,
        out_specs=out_specs,
        out_shape=out_shape,
        scratch_shapes=scratch_shapes,
        compiler_params=pltpu.CompilerParams(
            dimension_semantics=("arbitrary", "arbitrary"),
            vmem_limit_bytes=VMEM_LIMIT_BYTES),
        name=f"mixer_layer_t{seq}",
    )(*args)


def kernel(x_prompt, x_sample, state_conv, state_ret, norm_g, w_in, conv_w, w_a, w_r, w_o, final_g):
    log_gamma = jnp.log(1.0 - 2.0 ** (-5.0 - jnp.arange(RET_HEADS, dtype=F32)))
    bp, tp, _ = x_prompt.shape
    ts = x_sample.shape[1]
    cos_p, sin_p = _rope_tables(tp, 0)
    cos_s, sin_s = _rope_tables(ts, PAST_LEN)
    col = jnp.arange(D_IN)
    gate_cols = ((col >= OFF_ZA) & (col < OFF_Q)) | (col >= OFF_ZR)
    win = (w_in * jnp.where(gate_cols, 0.5, 1.0).astype(F32)).astype(BF16)
    wa = (0.5 * w_a).astype(BF16)
    wr = (0.5 * w_r).astype(BF16)
    wo = w_o.astype(BF16)
    conv_p0 = jnp.zeros((1, bp, CONV_W - 1, D_CONV), x_prompt.dtype)
    ret_p0 = jnp.zeros((1, bp, RET_HEADS, RET_DK, RET_DV), F32)

    xp, xs = x_prompt, x_sample
    conv_p, ret_p, conv_s, ret_s = [], [], [], []
    for layer in range(DEPTH):
        fg = final_g if layer == DEPTH - 1 else None
        xp, cp, rp = _mixer_layer(layer, 0, xp, cos_p, sin_p, conv_p0, ret_p0, log_gamma,
                                  norm_g, conv_w, win, wa, wr, wo, fg)
        xs, cs, rs = _mixer_layer(layer, layer, xs, cos_s, sin_s, state_conv, state_ret,
                                  log_gamma, norm_g, conv_w, win, wa, wr, wo, fg)
        conv_p.append(cp)
        ret_p.append(rp)
        conv_s.append(cs)
        ret_s.append(rs)
    return (xp, xs, jnp.stack(conv_p), jnp.stack(ret_p), jnp.stack(conv_s), jnp.stack(ret_s))
```

```python
import functools

import jax
import jax.numpy as jnp
from jax import lax
from jax.experimental import pallas as pl
from jax.experimental.pallas import tpu as pltpu

D_MODEL = 1024
DEPTH = 4
PAST_LEN = 1024
D_CONV = D_MODEL
CONV_W = 3
RET_HEADS = 4
RET_DK = D_MODEL // 8
RET_DV = D_MODEL // 4
D_QK = RET_HEADS * RET_DK
D_RV = RET_HEADS * RET_DV
ROPE_BASE = 10000.0
EPS = 1e-6

OFF_HC = 0
OFF_B = OFF_HC + D_CONV
OFF_C = OFF_B + D_CONV
OFF_ZA = OFF_C + D_CONV
OFF_Q = OFF_ZA + D_CONV
OFF_K = OFF_Q + D_QK
OFF_V = OFF_K + D_QK
OFF_ZR = OFF_V + D_RV
OFF_GA = OFF_ZR + D_RV
OFF_GR = OFF_GA + D_MODEL
D_IN = OFF_GR + D_MODEL

SUBLANES = 8
HALF = RET_DK // 2
MAX_CHUNK = 256
TILE_ROWS = 512
VMEM_LIMIT_BYTES = 60000 * 1024

ROW_GAIN = 0
ROW_CONV = 1
ROW_FINAL_GAIN = ROW_CONV + CONV_W
VEC_ROWS = SUBLANES
W_A, W_R, W_O = 0, 1, 2

F32 = jnp.float32
BF16 = jnp.bfloat16


def _rope_kernel(rope_ref, *, pos0, rows):
    shape = (rows, RET_DK)
    pos = lax.broadcasted_iota(jnp.int32, shape, 0) + (pl.program_id(0) * rows + pos0)
    lane = lax.broadcasted_iota(jnp.int32, shape, 1)
    idx = jnp.where(lane >= HALF, lane - HALF, lane).astype(F32)
    inv = 1.0 / (ROPE_BASE ** (idx / HALF))
    ang = pos.astype(F32) * inv
    sin = jnp.sin(ang)
    rope_ref[:, :RET_DK] = jnp.cos(ang)
    rope_ref[:, RET_DK:] = jnp.where(lane >= HALF, sin, -sin)


def _rope_table(seq, pos0):
    rows = min(seq, 512)
    return pl.pallas_call(
        functools.partial(_rope_kernel, pos0=pos0, rows=rows),
        grid=(seq // rows,),
        out_specs=pl.BlockSpec((rows, 2 * RET_DK), lambda i: (i, 0)),
        out_shape=jax.ShapeDtypeStruct((seq, 2 * RET_DK), F32),
        name="rope_table",
    )()


def _twice_sigmoid_of_twice(half_x):
    return jnp.tanh(half_x) + 1.0


def _rms_scale(x, gain):
    return x * lax.rsqrt(jnp.mean(x * x, axis=-1, keepdims=True) + EPS) * gain


def _layer_kernel(lg_ref, x_ref, rope_ref, conv0_ref, ret0_ref, vec_ref, win_ref, wsq_ref,
                  conv_all_ref, ret_all_ref, y_ref, convn_ref, retn_ref,
                  s_ref, ubuf_ref, dec_ref, qdec_ref, kdec_ref, *, nb, tt, chunk, final_norm):
    del conv_all_ref, ret_all_ref
    j = pl.program_id(1)
    nj = pl.num_programs(1)
    m = nb * tt

    @pl.when((pl.program_id(0) == 0) & (j == 0))
    def _init_decay_tables():
        row = lax.broadcasted_iota(jnp.int32, (chunk, chunk), 0)
        col = lax.broadcasted_iota(jnp.int32, (chunk, chunk), 1)
        rel = (row - col).astype(F32)
        rq = lax.broadcasted_iota(jnp.int32, (chunk, RET_DV), 0).astype(F32)
        rk = lax.broadcasted_iota(jnp.int32, (chunk, RET_DK), 0).astype(F32)
        for h in range(RET_HEADS):
            lg = lg_ref[h]
            dec_ref[h] = jnp.where(rel >= 0, jnp.exp(lg * jnp.maximum(rel, 0.0)), 0.0)
            qdec_ref[h] = jnp.exp(lg * (rq + 1.0))
            kdec_ref[h] = jnp.exp(lg * (chunk - 1.0 - rk))

    @pl.when(j == 0)
    def _load_state():
        s_ref[...] = ret0_ref[...]
        for s in range(nb):
            ubuf_ref[s, SUBLANES - 2:SUBLANES, :] = conv0_ref[s]

    xs = x_ref[...].reshape(m, D_MODEL)
    hb = _rms_scale(xs, vec_ref[ROW_GAIN:ROW_GAIN + 1, :]).astype(BF16)

    def proj(off, n):
        return jnp.dot(hb, win_ref[:, off:off + n], preferred_element_type=F32)

    def tap(i):
        return vec_ref[ROW_CONV + i:ROW_CONV + i + 1, :]

    u = proj(OFF_C, D_CONV) * proj(OFF_HC, D_CONV)
    gate_a = _twice_sigmoid_of_twice(proj(OFF_GA, D_MODEL))
    q = proj(OFF_Q, D_QK)
    k = proj(OFF_K, D_QK)
    v = proj(OFF_V, D_RV).astype(BF16)
    convs = []
    for s in range(nb):
        us = u[s * tt:(s + 1) * tt]
        ubuf_ref[s, SUBLANES:SUBLANES + tt, :] = us
        u1 = ubuf_ref[s, SUBLANES - 1:SUBLANES - 1 + tt, :]
        u2 = ubuf_ref[s, SUBLANES - 2:SUBLANES - 2 + tt, :]
        convs.append(tap(0) * u2 + tap(1) * u1 + tap(2) * us)
    conv = convs[0] if nb == 1 else jnp.concatenate(convs, axis=0)
    half_za = proj(OFF_ZA, D_CONV)
    silu_za = half_za * _twice_sigmoid_of_twice(half_za)
    a_in = ((proj(OFF_B, D_CONV) * conv) * silu_za).astype(BF16)

    for s in range(nb):
        ubuf_ref[s, SUBLANES - 2:SUBLANES, :] = ubuf_ref[s, tt + SUBLANES - 2:tt + SUBLANES, :]

    merged = gate_a * jnp.dot(a_in, wsq_ref[W_A], preferred_element_type=F32)
    cos_t = rope_ref[:, :RET_DK]
    sin_t = rope_ref[:, RET_DK:]
    if nb > 1:
        cos_t = jnp.concatenate([cos_t] * nb, axis=0)
        sin_t = jnp.concatenate([sin_t] * nb, axis=0)

    def rotary(xh):
        return xh * cos_t + pltpu.roll(xh, HALF, 1) * sin_t

    blocks = [(h, s, s * tt + c * chunk) for c in range(tt // chunk) for s in range(nb)
              for h in range(RET_HEADS)]
    qb, kf, scores = {}, {}, {}
    for h in range(RET_HEADS):
        qb[h] = rotary(q[:, h * RET_DK:(h + 1) * RET_DK]).astype(BF16)
        kf[h] = rotary(k[:, h * RET_DK:(h + 1) * RET_DK]) * (RET_DK ** -0.5)
    for h, s, r0 in blocks:
        sc = lax.dot_general(qb[h][r0:r0 + chunk], kf[h][r0:r0 + chunk].astype(BF16),
                             (((1,), (1,)), ((), ())), preferred_element_type=F32)
        scores[h, r0] = (sc * dec_ref[h]).astype(BF16)
    half_zr = proj(OFF_ZR, D_RV)
    silu_zr = half_zr * _twice_sigmoid_of_twice(half_zr)
    o_blocks, state = {}, {}
    for h, s, r0 in blocks:
        vc = v[r0:r0 + chunk, h * RET_DV:(h + 1) * RET_DV]
        st = state.get((h, s))
        if st is None:
            st = s_ref[s, h]
        o_cross = jnp.dot(qb[h][r0:r0 + chunk], st.astype(BF16),
                          preferred_element_type=F32) * qdec_ref[h]
        o = jnp.dot(scores[h, r0], vc, preferred_element_type=F32) + o_cross
        kd = (kf[h][r0:r0 + chunk] * kdec_ref[h]).astype(BF16)
        g_chunk = jnp.exp(jnp.full((1, RET_DV), lg_ref[h] * chunk, F32))
        state[h, s] = g_chunk * st + lax.dot_general(
            kd, vc, (((0,), (0,)), ((), ())), preferred_element_type=F32)
        o_blocks[h, r0] = o * lax.rsqrt(jnp.mean(o * o, axis=-1, keepdims=True) + EPS)
    for (h, s), st in state.items():
        s_ref[s, h] = st
    row_starts = sorted({r0 for _, _, r0 in blocks})
    o_all = jnp.concatenate(
        [jnp.concatenate([o_blocks[h, r0] for h in range(RET_HEADS)], axis=1)
         for r0 in row_starts], axis=0)
    o_gated = (o_all * silu_zr).astype(BF16)
    gate_r = _twice_sigmoid_of_twice(proj(OFF_GR, D_MODEL))
    half_r = jnp.dot(o_gated, wsq_ref[W_R], preferred_element_type=F32)
    merged = (merged + gate_r * half_r).astype(BF16)

    out = xs + jnp.dot(merged, wsq_ref[W_O], preferred_element_type=F32)
    if final_norm:
        out = _rms_scale(out, vec_ref[ROW_FINAL_GAIN:ROW_FINAL_GAIN + 1, :])
    y_ref[...] = out.reshape(nb, tt, D_MODEL)

    @pl.when(j == nj - 1)
    def _store_states():
        retn_ref[...] = s_ref[...]
        for s in range(nb):
            convn_ref[s] = ubuf_ref[s, SUBLANES - 2:SUBLANES, :]


def _tile_plan(seq):
    tt = min(seq, TILE_ROWS)
    nb = max(1, 256 // tt)
    return nb, tt


def _mixer_layer(layer, state_layer, final_norm, x, rope, conv0, ret0, log_gamma, vecs, win, wsq,
                 conv_all, ret_all):
    batch, seq, _ = x.shape
    nb, tt = _tile_plan(seq)
    chunk = min(tt, MAX_CHUNK)

    def layer_spec(shape):
        zeros = (0,) * len(shape)
        return pl.BlockSpec((None,) + shape, lambda b, j: (layer,) + zeros,
                            pipeline_mode=pl.Buffered(1))

    in_specs = [
        pl.BlockSpec(memory_space=pltpu.SMEM),
        pl.BlockSpec((nb, tt, D_MODEL), lambda b, j: (b, j, 0)),
        pl.BlockSpec((tt, 2 * RET_DK), lambda b, j: (j, 0)),
        pl.BlockSpec((None, nb, CONV_W - 1, D_CONV), lambda b, j: (state_layer, b, 0, 0)),
        pl.BlockSpec((None, nb, RET_HEADS, RET_DK, RET_DV),
                     lambda b, j: (state_layer, b, 0, 0, 0)),
        layer_spec((VEC_ROWS, D_MODEL)),
        layer_spec((D_MODEL, D_IN)),
        layer_spec((3, D_MODEL, D_MODEL)),
        pl.BlockSpec(memory_space=pl.ANY),
        pl.BlockSpec(memory_space=pl.ANY),
    ]
    args = (log_gamma, x, rope, conv0, ret0, vecs, win, wsq, conv_all, ret_all)
    out_shape = (
        jax.ShapeDtypeStruct(x.shape, x.dtype),
        jax.ShapeDtypeStruct(conv_all.shape, conv_all.dtype),
        jax.ShapeDtypeStruct(ret_all.shape, ret_all.dtype),
    )
    out_specs = (
        pl.BlockSpec((nb, tt, D_MODEL), lambda b, j: (b, j, 0)),
        pl.BlockSpec((None, nb, CONV_W - 1, D_CONV), lambda b, j: (layer, b, 0, 0)),
        pl.BlockSpec((None, nb, RET_HEADS, RET_DK, RET_DV), lambda b, j: (layer, b, 0, 0, 0)),
    )
    scratch_shapes = [
        pltpu.VMEM((nb, RET_HEADS, RET_DK, RET_DV), F32),
        pltpu.VMEM((nb, tt + SUBLANES, D_CONV), F32),
        pltpu.VMEM((RET_HEADS, chunk, chunk), F32),
        pltpu.VMEM((RET_HEADS, chunk, RET_DV), F32),
        pltpu.VMEM((RET_HEADS, chunk, RET_DK), F32),
    ]
    return pl.pallas_call(
        functools.partial(_layer_kernel, nb=nb, tt=tt, chunk=chunk, final_norm=final_norm),
        grid=(batch // nb, seq // tt),
        in_specs=in_specs,
        out_specs=out_specs,
        out_shape=out_shape,
        scratch_shapes=scratch_shapes,
        input_output_aliases={len(args) - 2: 1, len(args) - 1: 2},
        compiler_params=pltpu.CompilerParams(
            dimension_semantics=("arbitrary", "arbitrary"),
            vmem_limit_bytes=VMEM_LIMIT_BYTES),
        name=f"mixer_layer_t{seq}",
    )(*args)


def kernel(x_prompt, x_sample, state_conv, state_ret, norm_g, w_in, conv_w, w_a, w_r, w_o, final_g):
    log_gamma = jnp.log(1.0 - 2.0 ** (-5.0 - jnp.arange(RET_HEADS, dtype=F32)))
    bp, tp, _ = x_prompt.shape
    ts = x_sample.shape[1]
    rope_p = _rope_table(tp, 0)
    rope_s = _rope_table(ts, PAST_LEN)
    col = jnp.arange(D_IN)
    gate_cols = ((col >= OFF_ZA) & (col < OFF_Q)) | (col >= OFF_ZR)
    win = (w_in * jnp.where(gate_cols, 0.5, 1.0).astype(F32)).astype(BF16)
    wsq = jnp.stack([0.5 * w_a, 0.5 * w_r, w_o], axis=1).astype(BF16)
    vecs = jnp.concatenate(
        [norm_g[:, None, :], conv_w, jnp.broadcast_to(final_g, (DEPTH, 1, D_MODEL)),
         jnp.zeros((DEPTH, VEC_ROWS - ROW_FINAL_GAIN - 1, D_MODEL), F32)], axis=1)
    conv_p0 = jnp.zeros((1, bp, CONV_W - 1, D_CONV), x_prompt.dtype)
    ret_p0 = jnp.zeros((1, bp, RET_HEADS, RET_DK, RET_DV), F32)

    xp, xs = x_prompt, x_sample
    conv_p = jnp.zeros((DEPTH,) + conv_p0.shape[1:], conv_p0.dtype)
    ret_p = jnp.zeros((DEPTH,) + ret_p0.shape[1:], F32)
    conv_s = jnp.zeros_like(state_conv)
    ret_s = jnp.zeros_like(state_ret)
    for layer in range(DEPTH):
        final_norm = layer == DEPTH - 1
        xp, conv_p, ret_p = _mixer_layer(layer, 0, final_norm, xp, rope_p, conv_p0, ret_p0,
                                         log_gamma, vecs, win, wsq, conv_p, ret_p)
        xs, conv_s, ret_s = _mixer_layer(layer, layer, final_norm, xs, rope_s, state_conv,
                                         state_ret, log_gamma, vecs, win, wsq, conv_s, ret_s)
    return (xp, xs, conv_p, ret_p, conv_s, ret_s)
```

```python
import functools

import jax
import jax.numpy as jnp
from jax import lax
from jax.experimental import pallas as pl
from jax.experimental.pallas import tpu as pltpu

D_MODEL = 1024
DEPTH = 4
PAST_LEN = 1024
D_CONV = D_MODEL
CONV_W = 3
RET_HEADS = 4
RET_DK = D_MODEL // 8
RET_DV = D_MODEL // 4
D_QK = RET_HEADS * RET_DK
D_RV = RET_HEADS * RET_DV
ROPE_BASE = 10000.0
EPS = 1e-6

OFF_HC = 0
OFF_B = OFF_HC + D_CONV
OFF_C = OFF_B + D_CONV
OFF_ZA = OFF_C + D_CONV
OFF_Q = OFF_ZA + D_CONV
OFF_K = OFF_Q + D_QK
OFF_V = OFF_K + D_QK
OFF_ZR = OFF_V + D_RV
OFF_GA = OFF_ZR + D_RV
OFF_GR = OFF_GA + D_MODEL
D_IN = OFF_GR + D_MODEL

SUBLANES = 8
LANES = 128
HALF = RET_DK // 2
MAX_CHUNK = 256
TILE_ROWS = 512
VMEM_LIMIT_BYTES = 60000 * 1024

ROW_GAIN = 0
ROW_CONV = 1
ROW_FINAL_GAIN = ROW_CONV + CONV_W
VEC_ROWS = SUBLANES
W_A, W_R, W_O = 0, 1, 2

F32 = jnp.float32
BF16 = jnp.bfloat16


def _rope_kernel(rope_ref, *, pos0, rows):
    shape = (rows, RET_DK)
    pos = lax.broadcasted_iota(jnp.int32, shape, 0) + (pl.program_id(0) * rows + pos0)
    lane = lax.broadcasted_iota(jnp.int32, shape, 1)
    idx = jnp.where(lane >= HALF, lane - HALF, lane).astype(F32)
    inv = 1.0 / (ROPE_BASE ** (idx / HALF))
    ang = pos.astype(F32) * inv
    sin = jnp.sin(ang)
    rope_ref[:, :RET_DK] = jnp.cos(ang)
    rope_ref[:, RET_DK:] = jnp.where(lane >= HALF, sin, -sin)


def _rope_table(seq, pos0):
    rows = min(seq, 512)
    return pl.pallas_call(
        functools.partial(_rope_kernel, pos0=pos0, rows=rows),
        grid=(seq // rows,),
        out_specs=pl.BlockSpec((rows, 2 * RET_DK), lambda i: (i, 0)),
        out_shape=jax.ShapeDtypeStruct((seq, 2 * RET_DK), F32),
        name="rope_table",
    )()


def _twice_sigmoid_of_twice(half_x):
    return jnp.tanh(half_x) + 1.0


def _rms_scale(x, gain):
    return x * lax.rsqrt(jnp.mean(x * x, axis=-1, keepdims=True) + EPS) * gain


def _layer_kernel(lg_ref, x_ref, rope_ref, conv0_ref, ret0_ref, vec_ref, win_ref, wsq_ref,
                  conv_all_ref, ret_all_ref, y_ref, convn_ref, retn_ref,
                  s_ref, ubuf_ref, dec_ref, qdec_ref, kdec_ref, *, nb, tt, chunk, final_norm):
    del conv_all_ref, ret_all_ref
    j = pl.program_id(1)
    nj = pl.num_programs(1)
    m = nb * tt

    @pl.when((pl.program_id(0) == 0) & (j == 0))
    def _init_decay_tables():
        row = lax.broadcasted_iota(jnp.int32, (chunk, chunk), 0)
        col = lax.broadcasted_iota(jnp.int32, (chunk, chunk), 1)
        rel = (row - col).astype(F32)
        rq = lax.broadcasted_iota(jnp.int32, (chunk, RET_DV), 0).astype(F32)
        rk = lax.broadcasted_iota(jnp.int32, (chunk, RET_DK), 0).astype(F32)
        for h in range(RET_HEADS):
            lg = lg_ref[h]
            dec_ref[h] = jnp.where(rel >= 0, jnp.exp(lg * jnp.maximum(rel, 0.0)), 0.0)
            qdec_ref[h] = jnp.exp(lg * (rq + 1.0))
            kdec_ref[h] = jnp.exp(lg * (chunk - 1.0 - rk))

    @pl.when(j == 0)
    def _load_state():
        s_ref[...] = ret0_ref[...]
        for s in range(nb):
            for c in range(D_CONV // LANES):
                ubuf_ref[s, c, SUBLANES - 2:SUBLANES, :] = conv0_ref[s, :, c * LANES:(c + 1) * LANES]

    xs = x_ref[...].reshape(m, D_MODEL)
    hb = _rms_scale(xs, vec_ref[ROW_GAIN:ROW_GAIN + 1, :]).astype(BF16)

    def proj(off, n):
        return jnp.dot(hb, win_ref[:, off:off + n], preferred_element_type=F32)

    def tap(i):
        return vec_ref[ROW_CONV + i:ROW_CONV + i + 1, :]

    first = jnp.concatenate(
        [jnp.dot(hb[r:r + 128], win_ref[:, OFF_C:OFF_C + D_CONV], preferred_element_type=F32)
         for r in range(0, m, 128)], axis=0)
    u = first * proj(OFF_HC, D_CONV)
    gate_a = _twice_sigmoid_of_twice(proj(OFF_GA, D_MODEL))
    q = proj(OFF_Q, D_QK)
    k = proj(OFF_K, D_QK)
    v = proj(OFF_V, D_RV).astype(BF16)
    convs = []
    for s in range(nb):
        us = u[s * tt:(s + 1) * tt]
        for c in range(D_CONV // LANES):
            ubuf_ref[s, c, SUBLANES:SUBLANES + tt, :] = us[:, c * LANES:(c + 1) * LANES]

        def delayed(d):
            return jnp.concatenate(
                [ubuf_ref[s, c, SUBLANES - d:SUBLANES - d + tt, :]
                 for c in range(D_CONV // LANES)], axis=1)

        convs.append(tap(0) * delayed(2) + tap(1) * delayed(1) + tap(2) * us)
    conv = convs[0] if nb == 1 else jnp.concatenate(convs, axis=0)
    half_za = proj(OFF_ZA, D_CONV)
    silu_za = half_za * _twice_sigmoid_of_twice(half_za)
    a_in = ((proj(OFF_B, D_CONV) * conv) * silu_za).astype(BF16)

    for s in range(nb):
        ubuf_ref[s, :, SUBLANES - 2:SUBLANES, :] = ubuf_ref[s, :, tt + SUBLANES - 2:tt + SUBLANES, :]

    merged = gate_a * jnp.dot(a_in, wsq_ref[W_A], preferred_element_type=F32)
    cos_t = rope_ref[:, :RET_DK]
    sin_t = rope_ref[:, RET_DK:]
    if nb > 1:
        cos_t = jnp.concatenate([cos_t] * nb, axis=0)
        sin_t = jnp.concatenate([sin_t] * nb, axis=0)

    def rotary(xh):
        return xh * cos_t + pltpu.roll(xh, HALF, 1) * sin_t

    blocks = [(h, s, s * tt + c * chunk) for c in range(tt // chunk) for s in range(nb)
              for h in range(RET_HEADS)]
    qb, kf, scores = {}, {}, {}
    for h in range(RET_HEADS):
        qb[h] = rotary(q[:, h * RET_DK:(h + 1) * RET_DK]).astype(BF16)
        kf[h] = rotary(k[:, h * RET_DK:(h + 1) * RET_DK]) * (RET_DK ** -0.5)
    for h, s, r0 in blocks:
        sc = lax.dot_general(qb[h][r0:r0 + chunk], kf[h][r0:r0 + chunk].astype(BF16),
                             (((1,), (1,)), ((), ())), preferred_element_type=F32)
        scores[h, r0] = (sc * dec_ref[h]).astype(BF16)
    o_blocks, state = {}, {}
    for h, s, r0 in blocks:
        vc = v[r0:r0 + chunk, h * RET_DV:(h + 1) * RET_DV]
        st = state.get((h, s))
        if st is None:
            st = s_ref[s, h]
        o_cross = jnp.dot(qb[h][r0:r0 + chunk], st.astype(BF16),
                          preferred_element_type=F32) * qdec_ref[h]
        o = jnp.dot(scores[h, r0], vc, preferred_element_type=F32) + o_cross
        kd = (kf[h][r0:r0 + chunk] * kdec_ref[h]).astype(BF16)
        g_chunk = jnp.exp(jnp.full((1, RET_DV), lg_ref[h] * chunk, F32))
        state[h, s] = g_chunk * st + lax.dot_general(
            kd, vc, (((0,), (0,)), ((), ())), preferred_element_type=F32)
        o_blocks[h, r0] = o * lax.rsqrt(jnp.mean(o * o, axis=-1, keepdims=True) + EPS)
    for (h, s), st in state.items():
        s_ref[s, h] = st
    row_starts = sorted({r0 for _, _, r0 in blocks})
    o_all = jnp.concatenate(
        [jnp.concatenate([o_blocks[h, r0] for h in range(RET_HEADS)], axis=1)
         for r0 in row_starts], axis=0)
    half_zr = proj(OFF_ZR, D_RV)
    silu_zr = half_zr * _twice_sigmoid_of_twice(half_zr)
    o_gated = (o_all * silu_zr).astype(BF16)
    gate_r = _twice_sigmoid_of_twice(proj(OFF_GR, D_MODEL))
    half_r = jnp.dot(o_gated, wsq_ref[W_R], preferred_element_type=F32)
    merged = (merged + gate_r * half_r).astype(BF16)

    out = xs + jnp.dot(merged, wsq_ref[W_O], preferred_element_type=F32)
    if final_norm:
        out = _rms_scale(out, vec_ref[ROW_FINAL_GAIN:ROW_FINAL_GAIN + 1, :])
    y_ref[...] = out.reshape(nb, tt, D_MODEL)

    @pl.when(j == nj - 1)
    def _store_states():
        retn_ref[...] = s_ref[...]
        for s in range(nb):
            for c in range(D_CONV // LANES):
                convn_ref[s, :, c * LANES:(c + 1) * LANES] = ubuf_ref[s, c, SUBLANES - 2:SUBLANES, :]


def _tile_plan(seq):
    tt = min(seq, TILE_ROWS)
    nb = max(1, 256 // tt)
    return nb, tt


def _mixer_layer(layer, state_layer, final_norm, x, rope, conv0, ret0, log_gamma, vecs, win, wsq,
                 conv_all, ret_all):
    batch, seq, _ = x.shape
    nb, tt = _tile_plan(seq)
    chunk = min(tt, MAX_CHUNK)

    def layer_spec(shape):
        zeros = (0,) * len(shape)
        return pl.BlockSpec((None,) + shape, lambda b, j: (layer,) + zeros,
                            pipeline_mode=pl.Buffered(1))

    in_specs = [
        pl.BlockSpec(memory_space=pltpu.SMEM),
        pl.BlockSpec((nb, tt, D_MODEL), lambda b, j: (b, j, 0)),
        pl.BlockSpec((tt, 2 * RET_DK), lambda b, j: (j, 0)),
        pl.BlockSpec((None, nb, CONV_W - 1, D_CONV), lambda b, j: (state_layer, b, 0, 0)),
        pl.BlockSpec((None, nb, RET_HEADS, RET_DK, RET_DV),
                     lambda b, j: (state_layer, b, 0, 0, 0)),
        layer_spec((VEC_ROWS, D_MODEL)),
        layer_spec((D_MODEL, D_IN)),
        layer_spec((3, D_MODEL, D_MODEL)),
        pl.BlockSpec(memory_space=pl.ANY),
        pl.BlockSpec(memory_space=pl.ANY),
    ]
    args = (log_gamma, x, rope, conv0, ret0, vecs, win, wsq, conv_all, ret_all)
    out_shape = (
        jax.ShapeDtypeStruct(x.shape, x.dtype),
        jax.ShapeDtypeStruct(conv_all.shape, conv_all.dtype),
        jax.ShapeDtypeStruct(ret_all.shape, ret_all.dtype),
    )
    out_specs = (
        pl.BlockSpec((nb, tt, D_MODEL), lambda b, j: (b, j, 0)),
        pl.BlockSpec((None, nb, CONV_W - 1, D_CONV), lambda b, j: (layer, b, 0, 0)),
        pl.BlockSpec((None, nb, RET_HEADS, RET_DK, RET_DV), lambda b, j: (layer, b, 0, 0, 0)),
    )
    scratch_shapes = [
        pltpu.VMEM((nb, RET_HEADS, RET_DK, RET_DV), F32),
        pltpu.VMEM((nb, D_CONV // LANES, tt + SUBLANES, LANES), F32),
        pltpu.VMEM((RET_HEADS, chunk, chunk), F32),
        pltpu.VMEM((RET_HEADS, chunk, RET_DV), F32),
        pltpu.VMEM((RET_HEADS, chunk, RET_DK), F32),
    ]
    return pl.pallas_call(
        functools.partial(_layer_kernel, nb=nb, tt=tt, chunk=chunk, final_norm=final_norm),
        grid=(batch // nb, seq // tt),
        in_specs=in_specs,
        out_specs=out_specs,
        out_shape=out_shape,
        scratch_shapes=scratch_shapes,
        input_output_aliases={len(args) - 2: 1, len(args) - 1: 2},
        compiler_params=pltpu.CompilerParams(
            dimension_semantics=("arbitrary", "arbitrary"),
            vmem_limit_bytes=VMEM_LIMIT_BYTES),
        name=f"mixer_layer_t{seq}",
    )(*args)


def kernel(x_prompt, x_sample, state_conv, state_ret, norm_g, w_in, conv_w, w_a, w_r, w_o, final_g):
    log_gamma = jnp.log(1.0 - 2.0 ** (-5.0 - jnp.arange(RET_HEADS, dtype=F32)))
    bp, tp, _ = x_prompt.shape
    ts = x_sample.shape[1]
    rope_p = _rope_table(tp, 0)
    rope_s = _rope_table(ts, PAST_LEN)
    col = jnp.arange(D_IN)
    gate_cols = ((col >= OFF_ZA) & (col < OFF_Q)) | (col >= OFF_ZR)
    win = (w_in * jnp.where(gate_cols, 0.5, 1.0).astype(F32)).astype(BF16)
    wsq = jnp.stack([0.5 * w_a, 0.5 * w_r, w_o], axis=1).astype(BF16)
    vecs = jnp.concatenate(
        [norm_g[:, None, :], conv_w, jnp.broadcast_to(final_g, (DEPTH, 1, D_MODEL)),
         jnp.zeros((DEPTH, VEC_ROWS - ROW_FINAL_GAIN - 1, D_MODEL), F32)], axis=1)
    conv_p0 = jnp.zeros((1, bp, CONV_W - 1, D_CONV), x_prompt.dtype)
    ret_p0 = jnp.zeros((1, bp, RET_HEADS, RET_DK, RET_DV), F32)

    xp, xs = x_prompt, x_sample
    conv_p = jnp.zeros((DEPTH,) + conv_p0.shape[1:], conv_p0.dtype)
    ret_p = jnp.zeros((DEPTH,) + ret_p0.shape[1:], F32)
    conv_s = jnp.zeros_like(state_conv)
    ret_s = jnp.zeros_like(state_ret)
    for layer in range(DEPTH):
        final_norm = layer == DEPTH - 1
        xp, conv_p, ret_p = _mixer_layer(layer, 0, final_norm, xp, rope_p, conv_p0, ret_p0,
                                         log_gamma, vecs, win, wsq, conv_p, ret_p)
        xs, conv_s, ret_s = _mixer_layer(layer, layer, final_norm, xs, rope_s, state_conv,
                                         state_ret, log_gamma, vecs, win, wsq, conv_s, ret_s)
    return (xp, xs, conv_p, ret_p, conv_s, ret_s)
```

```python
import functools

import jax
import jax.numpy as jnp
from jax import lax
from jax.experimental import pallas as pl
from jax.experimental.pallas import tpu as pltpu

D_MODEL = 1024
DEPTH = 4
PAST_LEN = 1024
D_CONV = D_MODEL
CONV_W = 3
RET_HEADS = 4
RET_DK = D_MODEL // 8
RET_DV = D_MODEL // 4
D_QK = RET_HEADS * RET_DK
D_RV = RET_HEADS * RET_DV
ROPE_BASE = 10000.0
EPS = 1e-6

OFF_HC = 0
OFF_B = OFF_HC + D_CONV
OFF_C = OFF_B + D_CONV
OFF_ZA = OFF_C + D_CONV
OFF_Q = OFF_ZA + D_CONV
OFF_K = OFF_Q + D_QK
OFF_V = OFF_K + D_QK
OFF_ZR = OFF_V + D_RV
OFF_GA = OFF_ZR + D_RV
OFF_GR = OFF_GA + D_MODEL
D_IN = OFF_GR + D_MODEL

SUBLANES = 8
LANES = 128
HALF = RET_DK // 2
MAX_CHUNK = 256
TILE_ROWS = 512
MIN_TILE_ROWS = 256
FIRST_DOT_ROWS = 128
ROPE_ROWS = 512
VMEM_LIMIT_BYTES = 60000 * 1024

ROW_GAIN = 0
ROW_CONV = 1
ROW_FINAL_GAIN = ROW_CONV + CONV_W
VEC_ROWS = SUBLANES
W_A, W_R, W_O = 0, 1, 2

F32 = jnp.float32
BF16 = jnp.bfloat16


def _rope_kernel(rope_ref, *, pos0, rows):
    shape = (rows, RET_DK)
    pos = lax.broadcasted_iota(jnp.int32, shape, 0) + (pl.program_id(0) * rows + pos0)
    lane = lax.broadcasted_iota(jnp.int32, shape, 1)
    idx = jnp.where(lane >= HALF, lane - HALF, lane).astype(F32)
    inv = 1.0 / (ROPE_BASE ** (idx / HALF))
    ang = pos.astype(F32) * inv
    sin = jnp.sin(ang)
    rope_ref[:, :RET_DK] = jnp.cos(ang)
    rope_ref[:, RET_DK:] = jnp.where(lane >= HALF, sin, -sin)


def _rope_table(seq, pos0):
    rows = min(seq, ROPE_ROWS)
    return pl.pallas_call(
        functools.partial(_rope_kernel, pos0=pos0, rows=rows),
        grid=(seq // rows,),
        out_specs=pl.BlockSpec((rows, 2 * RET_DK), lambda i: (i, 0)),
        out_shape=jax.ShapeDtypeStruct((seq, 2 * RET_DK), F32),
        name="rope_table",
    )()


def _twice_sigmoid_of_twice(half_x):
    return jnp.tanh(half_x) + 1.0


def _rms_scale(x, gain):
    return x * lax.rsqrt(jnp.mean(x * x, axis=-1, keepdims=True) + EPS) * gain


def _layer_kernel(lg_ref, x_ref, rope_ref, conv0_ref, ret0_ref, vec_ref, win_ref, wsq_ref,
                  conv_all_ref, ret_all_ref, y_ref, convn_ref, retn_ref,
                  s_ref, ubuf_ref, dec_ref, qdec_ref, kdec_ref, *, nb, tt, chunk, final_norm):
    del conv_all_ref, ret_all_ref
    j = pl.program_id(1)
    nj = pl.num_programs(1)
    m = nb * tt

    @pl.when((pl.program_id(0) == 0) & (j == 0))
    def _init_decay_tables():
        row = lax.broadcasted_iota(jnp.int32, (chunk, chunk), 0)
        col = lax.broadcasted_iota(jnp.int32, (chunk, chunk), 1)
        rel = (row - col).astype(F32)
        rq = lax.broadcasted_iota(jnp.int32, (chunk, RET_DV), 0).astype(F32)
        rk = lax.broadcasted_iota(jnp.int32, (chunk, RET_DK), 0).astype(F32)
        for h in range(RET_HEADS):
            lg = lg_ref[h]
            dec_ref[h] = jnp.where(rel >= 0, jnp.exp(lg * jnp.maximum(rel, 0.0)), 0.0)
            qdec_ref[h] = jnp.exp(lg * (rq + 1.0))
            kdec_ref[h] = jnp.exp(lg * (chunk - 1.0 - rk))

    @pl.when(j == 0)
    def _load_state():
        s_ref[...] = ret0_ref[...]
        for s in range(nb):
            for c in range(D_CONV // LANES):
                lanes = slice(c * LANES, (c + 1) * LANES)
                ubuf_ref[s, c, SUBLANES - 2:SUBLANES, :] = conv0_ref[s, :, lanes]

    xs = x_ref[...].reshape(m, D_MODEL)
    hb = _rms_scale(xs, vec_ref[ROW_GAIN:ROW_GAIN + 1, :]).astype(BF16)

    def proj(off, n):
        return jnp.dot(hb, win_ref[:, off:off + n], preferred_element_type=F32)

    def tap(i):
        return vec_ref[ROW_CONV + i:ROW_CONV + i + 1, :]

    c_proj = jnp.concatenate(
        [jnp.dot(hb[r:r + FIRST_DOT_ROWS], win_ref[:, OFF_C:OFF_C + D_CONV],
                 preferred_element_type=F32) for r in range(0, m, FIRST_DOT_ROWS)], axis=0)
    u = c_proj * proj(OFF_HC, D_CONV)
    gate_a = _twice_sigmoid_of_twice(proj(OFF_GA, D_MODEL))
    q = proj(OFF_Q, D_QK)
    k = proj(OFF_K, D_QK)
    v = proj(OFF_V, D_RV).astype(BF16)
    convs = []
    for s in range(nb):
        us = u[s * tt:(s + 1) * tt]
        for c in range(D_CONV // LANES):
            ubuf_ref[s, c, SUBLANES:SUBLANES + tt, :] = us[:, c * LANES:(c + 1) * LANES]

        def delayed(d):
            return jnp.concatenate(
                [ubuf_ref[s, c, SUBLANES - d:SUBLANES - d + tt, :]
                 for c in range(D_CONV // LANES)], axis=1)

        convs.append(tap(0) * delayed(2) + tap(1) * delayed(1) + tap(2) * us)
    conv = convs[0] if nb == 1 else jnp.concatenate(convs, axis=0)
    half_za = proj(OFF_ZA, D_CONV)
    silu_za = half_za * _twice_sigmoid_of_twice(half_za)
    a_in = ((proj(OFF_B, D_CONV) * conv) * silu_za).astype(BF16)

    ubuf_ref[:, :, SUBLANES - 2:SUBLANES, :] = ubuf_ref[:, :, tt + SUBLANES - 2:tt + SUBLANES, :]

    merged = gate_a * jnp.dot(a_in, wsq_ref[W_A], preferred_element_type=F32)
    cos_t = rope_ref[:, :RET_DK]
    sin_t = rope_ref[:, RET_DK:]
    if nb > 1:
        cos_t = jnp.concatenate([cos_t] * nb, axis=0)
        sin_t = jnp.concatenate([sin_t] * nb, axis=0)

    def rotary(xh):
        return xh * cos_t + pltpu.roll(xh, HALF, 1) * sin_t

    blocks = [(h, s, s * tt + c * chunk) for c in range(tt // chunk) for s in range(nb)
              for h in range(RET_HEADS)]
    qb, kf, scores = {}, {}, {}
    for h in range(RET_HEADS):
        qb[h] = rotary(q[:, h * RET_DK:(h + 1) * RET_DK]).astype(BF16)
        kf[h] = rotary(k[:, h * RET_DK:(h + 1) * RET_DK]) * (RET_DK ** -0.5)
    for h, s, r0 in blocks:
        sc = lax.dot_general(qb[h][r0:r0 + chunk], kf[h][r0:r0 + chunk].astype(BF16),
                             (((1,), (1,)), ((), ())), preferred_element_type=F32)
        scores[h, r0] = (sc * dec_ref[h]).astype(BF16)
    o_blocks, state = {}, {}
    for h, s, r0 in blocks:
        vc = v[r0:r0 + chunk, h * RET_DV:(h + 1) * RET_DV]
        st = state.get((h, s))
        if st is None:
            st = s_ref[s, h]
        o_cross = jnp.dot(qb[h][r0:r0 + chunk], st.astype(BF16),
                          preferred_element_type=F32) * qdec_ref[h]
        o = jnp.dot(scores[h, r0], vc, preferred_element_type=F32) + o_cross
        kd = (kf[h][r0:r0 + chunk] * kdec_ref[h]).astype(BF16)
        g_chunk = jnp.exp(jnp.full((1, RET_DV), lg_ref[h] * chunk, F32))
        state[h, s] = g_chunk * st + lax.dot_general(
            kd, vc, (((0,), (0,)), ((), ())), preferred_element_type=F32)
        o_blocks[h, r0] = o * lax.rsqrt(jnp.mean(o * o, axis=-1, keepdims=True) + EPS)
    for (h, s), st in state.items():
        s_ref[s, h] = st
    row_starts = sorted({r0 for _, _, r0 in blocks})
    o_all = jnp.concatenate(
        [jnp.concatenate([o_blocks[h, r0] for h in range(RET_HEADS)], axis=1)
         for r0 in row_starts], axis=0)
    half_zr = proj(OFF_ZR, D_RV)
    silu_zr = half_zr * _twice_sigmoid_of_twice(half_zr)
    o_gated = (o_all * silu_zr).astype(BF16)
    gate_r = _twice_sigmoid_of_twice(proj(OFF_GR, D_MODEL))
    half_r = jnp.dot(o_gated, wsq_ref[W_R], preferred_element_type=F32)
    merged = (merged + gate_r * half_r).astype(BF16)

    out = xs + jnp.dot(merged, wsq_ref[W_O], preferred_element_type=F32)
    if final_norm:
        out = _rms_scale(out, vec_ref[ROW_FINAL_GAIN:ROW_FINAL_GAIN + 1, :])
    y_ref[...] = out.reshape(nb, tt, D_MODEL)

    @pl.when(j == nj - 1)
    def _store_states():
        retn_ref[...] = s_ref[...]
        for s in range(nb):
            for c in range(D_CONV // LANES):
                lanes = slice(c * LANES, (c + 1) * LANES)
                convn_ref[s, :, lanes] = ubuf_ref[s, c, SUBLANES - 2:SUBLANES, :]


def _tile_plan(seq):
    tt = min(seq, TILE_ROWS)
    nb = max(1, MIN_TILE_ROWS // tt)
    return nb, tt


def _mixer_layer(layer, state_layer, final_norm, x, rope, conv0, ret0, log_gamma, vecs, win, wsq,
                 conv_all, ret_all):
    batch, seq, _ = x.shape
    nb, tt = _tile_plan(seq)
    chunk = min(tt, MAX_CHUNK)

    def layer_spec(shape):
        zeros = (0,) * len(shape)
        return pl.BlockSpec((None,) + shape, lambda b, j: (layer,) + zeros,
                            pipeline_mode=pl.Buffered(1))

    in_specs = [
        pl.BlockSpec(memory_space=pltpu.SMEM),
        pl.BlockSpec((nb, tt, D_MODEL), lambda b, j: (b, j, 0)),
        pl.BlockSpec((tt, 2 * RET_DK), lambda b, j: (j, 0)),
        pl.BlockSpec((None, nb, CONV_W - 1, D_CONV), lambda b, j: (state_layer, b, 0, 0)),
        pl.BlockSpec((None, nb, RET_HEADS, RET_DK, RET_DV),
                     lambda b, j: (state_layer, b, 0, 0, 0)),
        layer_spec((VEC_ROWS, D_MODEL)),
        layer_spec((D_MODEL, D_IN)),
        layer_spec((3, D_MODEL, D_MODEL)),
        pl.BlockSpec(memory_space=pl.ANY),
        pl.BlockSpec(memory_space=pl.ANY),
    ]
    args = (log_gamma, x, rope, conv0, ret0, vecs, win, wsq, conv_all, ret_all)
    out_shape = (
        jax.ShapeDtypeStruct(x.shape, x.dtype),
        jax.ShapeDtypeStruct(conv_all.shape, conv_all.dtype),
        jax.ShapeDtypeStruct(ret_all.shape, ret_all.dtype),
    )
    out_specs = (
        pl.BlockSpec((nb, tt, D_MODEL), lambda b, j: (b, j, 0)),
        pl.BlockSpec((None, nb, CONV_W - 1, D_CONV), lambda b, j: (layer, b, 0, 0)),
        pl.BlockSpec((None, nb, RET_HEADS, RET_DK, RET_DV), lambda b, j: (layer, b, 0, 0, 0)),
    )
    scratch_shapes = [
        pltpu.VMEM((nb, RET_HEADS, RET_DK, RET_DV), F32),
        pltpu.VMEM((nb, D_CONV // LANES, tt + SUBLANES, LANES), F32),
        pltpu.VMEM((RET_HEADS, chunk, chunk), F32),
        pltpu.VMEM((RET_HEADS, chunk, RET_DV), F32),
        pltpu.VMEM((RET_HEADS, chunk, RET_DK), F32),
    ]
    return pl.pallas_call(
        functools.partial(_layer_kernel, nb=nb, tt=tt, chunk=chunk, final_norm=final_norm),
        grid=(batch // nb, seq // tt),
        in_specs=in_specs,
        out_specs=out_specs,
        out_shape=out_shape,
        scratch_shapes=scratch_shapes,
        input_output_aliases={len(args) - 2: 1, len(args) - 1: 2},
        compiler_params=pltpu.CompilerParams(
            dimension_semantics=("arbitrary", "arbitrary"),
            vmem_limit_bytes=VMEM_LIMIT_BYTES),
        name=f"mixer_layer_t{seq}",
    )(*args)


def kernel(x_prompt, x_sample, state_conv, state_ret, norm_g, w_in, conv_w, w_a, w_r, w_o, final_g):
    log_gamma = jnp.log(1.0 - 2.0 ** (-5.0 - jnp.arange(RET_HEADS, dtype=F32)))
    bp, tp, _ = x_prompt.shape
    ts = x_sample.shape[1]
    rope_p = _rope_table(tp, 0)
    rope_s = _rope_table(ts, PAST_LEN)
    col = jnp.arange(D_IN)
    gate_cols = ((col >= OFF_ZA) & (col < OFF_Q)) | (col >= OFF_ZR)
    win = (w_in * jnp.where(gate_cols, 0.5, 1.0).astype(F32)).astype(BF16)
    wsq = jnp.stack([0.5 * w_a, 0.5 * w_r, w_o], axis=1).astype(BF16)
    vecs = jnp.concatenate(
        [norm_g[:, None, :], conv_w, jnp.broadcast_to(final_g, (DEPTH, 1, D_MODEL)),
         jnp.zeros((DEPTH, VEC_ROWS - ROW_FINAL_GAIN - 1, D_MODEL), F32)], axis=1)
    conv_p0 = jnp.zeros((1, bp, CONV_W - 1, D_CONV), x_prompt.dtype)
    ret_p0 = jnp.zeros((1, bp, RET_HEADS, RET_DK, RET_DV), F32)

    xp, xs = x_prompt, x_sample
    conv_p = jnp.zeros((DEPTH,) + conv_p0.shape[1:], conv_p0.dtype)
    ret_p = jnp.zeros((DEPTH,) + ret_p0.shape[1:], F32)
    conv_s = jnp.zeros_like(state_conv)
    ret_s = jnp.zeros_like(state_ret)
    for layer in range(DEPTH):
        final_norm = layer == DEPTH - 1
        xp, conv_p, ret_p = _mixer_layer(layer, 0, final_norm, xp, rope_p, conv_p0, ret_p0,
                                         log_gamma, vecs, win, wsq, conv_p, ret_p)
        xs, conv_s, ret_s = _mixer_layer(layer, layer, final_norm, xs, rope_s, state_conv,
                                         state_ret, log_gamma, vecs, win, wsq, conv_s, ret_s)
    return (xp, xs, conv_p, ret_p, conv_s, ret_s)
```

```python
import functools

import jax
import jax.numpy as jnp
from jax import lax
from jax.experimental import pallas as pl
from jax.experimental.pallas import tpu as pltpu

D_MODEL = 1024
DEPTH = 4
PAST_LEN = 1024
D_CONV = D_MODEL
CONV_W = 3
RET_HEADS = 4
RET_DK = D_MODEL // 8
RET_DV = D_MODEL // 4
D_QK = RET_HEADS * RET_DK
D_RV = RET_HEADS * RET_DV
ROPE_BASE = 10000.0
EPS = 1e-6

OFF_HC = 0
OFF_B = OFF_HC + D_CONV
OFF_C = OFF_B + D_CONV
OFF_ZA = OFF_C + D_CONV
OFF_Q = OFF_ZA + D_CONV
OFF_K = OFF_Q + D_QK
OFF_V = OFF_K + D_QK
OFF_ZR = OFF_V + D_RV
OFF_GA = OFF_ZR + D_RV
OFF_GR = OFF_GA + D_MODEL
D_IN = OFF_GR + D_MODEL

SUBLANES = 8
LANES = 128
HALF = RET_DK // 2
MAX_CHUNK = 256
TILE_ROWS = 512
MIN_TILE_ROWS = 256
FIRST_DOT_ROWS = 128
ROPE_ROWS = 512
ROPE_TILES = 4
VMEM_LIMIT_BYTES = 60000 * 1024

ROW_GAIN = 0
ROW_CONV = 1
ROW_FINAL_GAIN = ROW_CONV + CONV_W
VEC_ROWS = SUBLANES
W_A, W_R, W_O = 0, 1, 2

F32 = jnp.float32
BF16 = jnp.bfloat16


def _rope_kernel(rope_ref, *, pos0, rows):
    shape = (rows, RET_DK)
    pos = lax.broadcasted_iota(jnp.int32, shape, 0) + (pl.program_id(0) * rows + pos0)
    lane = lax.broadcasted_iota(jnp.int32, shape, 1)
    idx = jnp.where(lane >= HALF, lane - HALF, lane).astype(F32)
    inv = 1.0 / (ROPE_BASE ** (idx / HALF))
    ang = pos.astype(F32) * inv
    sin = jnp.sin(ang)
    rope_ref[:, :RET_DK] = jnp.cos(ang)
    rope_ref[:, RET_DK:] = jnp.where(lane >= HALF, sin, -sin)


def _rope_table(seq, pos0):
    rows = min(seq, ROPE_ROWS)
    return pl.pallas_call(
        functools.partial(_rope_kernel, pos0=pos0, rows=rows),
        grid=(seq // rows,),
        out_specs=pl.BlockSpec((rows, 2 * RET_DK), lambda i: (i, 0)),
        out_shape=jax.ShapeDtypeStruct((seq, 2 * RET_DK), F32),
        name="rope_table",
    )()


def _twice_sigmoid_of_twice(half_x):
    return jnp.tanh(half_x) + 1.0


def _rms_scale(x, gain):
    return x * lax.rsqrt(jnp.mean(x * x, axis=-1, keepdims=True) + EPS) * gain


def _layer_kernel(lg_ref, x_ref, rope_ref, conv0_ref, ret0_ref, vec_ref, win_ref, wsq_ref,
                  conv_all_ref, ret_all_ref, y_ref, convn_ref, retn_ref,
                  s_ref, ubuf_ref, dec_ref, qdec_ref, kdec_ref, *, nb, tt, chunk, final_norm):
    del conv_all_ref, ret_all_ref
    j = pl.program_id(1)
    nj = pl.num_programs(1)
    m = nb * tt

    @pl.when((pl.program_id(0) == 0) & (j == 0))
    def _init_decay_tables():
        row = lax.broadcasted_iota(jnp.int32, (chunk, chunk), 0)
        col = lax.broadcasted_iota(jnp.int32, (chunk, chunk), 1)
        rel = (row - col).astype(F32)
        rq = lax.broadcasted_iota(jnp.int32, (chunk, RET_DV), 0).astype(F32)
        rk = lax.broadcasted_iota(jnp.int32, (chunk, RET_DK), 0).astype(F32)
        for h in range(RET_HEADS):
            lg = lg_ref[h]
            dec_ref[h] = jnp.where(rel >= 0, jnp.exp(lg * jnp.maximum(rel, 0.0)), 0.0)
            qdec_ref[h] = jnp.exp(lg * (rq + 1.0))
            kdec_ref[h] = jnp.exp(lg * (chunk - 1.0 - rk))

    @pl.when(j == 0)
    def _load_state():
        s_ref[...] = ret0_ref[...]
        for s in range(nb):
            for c in range(D_CONV // LANES):
                lanes = slice(c * LANES, (c + 1) * LANES)
                ubuf_ref[s, c, SUBLANES - 2:SUBLANES, :] = conv0_ref[s, :, lanes]

    xs = x_ref[...].reshape(m, D_MODEL)
    hb = _rms_scale(xs, vec_ref[ROW_GAIN:ROW_GAIN + 1, :]).astype(BF16)

    def proj(off, n):
        return jnp.dot(hb, win_ref[:, off:off + n], preferred_element_type=F32)

    def tap(i):
        return vec_ref[ROW_CONV + i:ROW_CONV + i + 1, :]

    c_proj = jnp.concatenate(
        [jnp.dot(hb[r:r + FIRST_DOT_ROWS], win_ref[:, OFF_C:OFF_C + D_CONV],
                 preferred_element_type=F32) for r in range(0, m, FIRST_DOT_ROWS)], axis=0)
    u = c_proj * proj(OFF_HC, D_CONV)
    gate_a = _twice_sigmoid_of_twice(proj(OFF_GA, D_MODEL))
    q = proj(OFF_Q, D_QK)
    k = proj(OFF_K, D_QK)
    v = proj(OFF_V, D_RV).astype(BF16)
    convs = []
    for s in range(nb):
        us = u[s * tt:(s + 1) * tt]
        for c in range(D_CONV // LANES):
            ubuf_ref[s, c, SUBLANES:SUBLANES + tt, :] = us[:, c * LANES:(c + 1) * LANES]

        def delayed(d):
            return jnp.concatenate(
                [ubuf_ref[s, c, SUBLANES - d:SUBLANES - d + tt, :]
                 for c in range(D_CONV // LANES)], axis=1)

        convs.append(tap(0) * delayed(2) + tap(1) * delayed(1) + tap(2) * us)
    conv = convs[0] if nb == 1 else jnp.concatenate(convs, axis=0)
    half_za = proj(OFF_ZA, D_CONV)
    silu_za = half_za * _twice_sigmoid_of_twice(half_za)
    a_in = ((proj(OFF_B, D_CONV) * conv) * silu_za).astype(BF16)

    ubuf_ref[:, :, SUBLANES - 2:SUBLANES, :] = ubuf_ref[:, :, tt + SUBLANES - 2:tt + SUBLANES, :]

    merged = gate_a * jnp.dot(a_in, wsq_ref[W_A], preferred_element_type=F32)
    if rope_ref.shape[0] == tt:
        rope_rows = slice(None)
    else:
        rope_rows = pl.ds(pl.multiple_of(lax.rem(j, rope_ref.shape[0] // tt) * tt, tt), tt)
    cos_t = rope_ref[rope_rows, :RET_DK]
    sin_t = rope_ref[rope_rows, RET_DK:]
    if nb > 1:
        cos_t = jnp.concatenate([cos_t] * nb, axis=0)
        sin_t = jnp.concatenate([sin_t] * nb, axis=0)

    def rotary(xh):
        return xh * cos_t + pltpu.roll(xh, HALF, 1) * sin_t

    blocks = [(h, s, s * tt + c * chunk) for c in range(tt // chunk) for s in range(nb)
              for h in range(RET_HEADS)]
    qb, kf, scores = {}, {}, {}
    for h in range(RET_HEADS):
        qb[h] = rotary(q[:, h * RET_DK:(h + 1) * RET_DK]).astype(BF16)
        kf[h] = rotary(k[:, h * RET_DK:(h + 1) * RET_DK]) * (RET_DK ** -0.5)
    for h, s, r0 in blocks:
        sc = lax.dot_general(qb[h][r0:r0 + chunk], kf[h][r0:r0 + chunk].astype(BF16),
                             (((1,), (1,)), ((), ())), preferred_element_type=F32)
        scores[h, r0] = (sc * dec_ref[h]).astype(BF16)
    o_blocks, state = {}, {}
    for h, s, r0 in blocks:
        vc = v[r0:r0 + chunk, h * RET_DV:(h + 1) * RET_DV]
        st = state.get((h, s))
        if st is None:
            st = s_ref[s, h]
        o_cross = jnp.dot(qb[h][r0:r0 + chunk], st.astype(BF16),
                          preferred_element_type=F32) * qdec_ref[h]
        o = jnp.dot(scores[h, r0], vc, preferred_element_type=F32) + o_cross
        kd = (kf[h][r0:r0 + chunk] * kdec_ref[h]).astype(BF16)
        g_chunk = jnp.exp(jnp.full((1, RET_DV), lg_ref[h] * chunk, F32))
        state[h, s] = g_chunk * st + lax.dot_general(
            kd, vc, (((0,), (0,)), ((), ())), preferred_element_type=F32)
        o_blocks[h, r0] = o * lax.rsqrt(jnp.mean(o * o, axis=-1, keepdims=True) + EPS)
    for (h, s), st in state.items():
        s_ref[s, h] = st
    row_starts = sorted({r0 for _, _, r0 in blocks})
    o_all = jnp.concatenate(
        [jnp.concatenate([o_blocks[h, r0] for h in range(RET_HEADS)], axis=1)
         for r0 in row_starts], axis=0)
    half_zr = proj(OFF_ZR, D_RV)
    silu_zr = half_zr * _twice_sigmoid_of_twice(half_zr)
    o_gated = (o_all * silu_zr).astype(BF16)
    gate_r = _twice_sigmoid_of_twice(proj(OFF_GR, D_MODEL))
    half_r = jnp.dot(o_gated, wsq_ref[W_R], preferred_element_type=F32)
    merged = (merged + gate_r * half_r).astype(BF16)

    out = xs + jnp.dot(merged, wsq_ref[W_O], preferred_element_type=F32)
    if final_norm:
        out = _rms_scale(out, vec_ref[ROW_FINAL_GAIN:ROW_FINAL_GAIN + 1, :])
    y_ref[...] = out.reshape(nb, tt, D_MODEL)

    @pl.when(j == nj - 1)
    def _store_states():
        retn_ref[...] = s_ref[...]
        for s in range(nb):
            for c in range(D_CONV // LANES):
                lanes = slice(c * LANES, (c + 1) * LANES)
                convn_ref[s, :, lanes] = ubuf_ref[s, c, SUBLANES - 2:SUBLANES, :]


def _tile_plan(seq):
    tt = min(seq, TILE_ROWS)
    nb = max(1, MIN_TILE_ROWS // tt)
    return nb, tt


def _mixer_layer(layer, state_layer, final_norm, x, rope, conv0, ret0, log_gamma, vecs, win, wsq,
                 conv_all, ret_all):
    batch, seq, _ = x.shape
    nb, tt = _tile_plan(seq)
    chunk = min(tt, MAX_CHUNK)
    rope_tiles = min(ROPE_TILES, seq // tt)

    def layer_spec(shape):
        zeros = (0,) * len(shape)
        return pl.BlockSpec((None,) + shape, lambda b, j: (layer,) + zeros,
                            pipeline_mode=pl.Buffered(1))

    in_specs = [
        pl.BlockSpec(memory_space=pltpu.SMEM),
        pl.BlockSpec((nb, tt, D_MODEL), lambda b, j: (b, j, 0)),
        pl.BlockSpec((rope_tiles * tt, 2 * RET_DK), lambda b, j: (j // rope_tiles, 0)),
        pl.BlockSpec((None, nb, CONV_W - 1, D_CONV), lambda b, j: (state_layer, b, 0, 0)),
        pl.BlockSpec((None, nb, RET_HEADS, RET_DK, RET_DV),
                     lambda b, j: (state_layer, b, 0, 0, 0)),
        layer_spec((VEC_ROWS, D_MODEL)),
        layer_spec((D_MODEL, D_IN)),
        layer_spec((3, D_MODEL, D_MODEL)),
        pl.BlockSpec(memory_space=pl.ANY),
        pl.BlockSpec(memory_space=pl.ANY),
    ]
    args = (log_gamma, x, rope, conv0, ret0, vecs, win, wsq, conv_all, ret_all)
    out_shape = (
        jax.ShapeDtypeStruct(x.shape, x.dtype),
        jax.ShapeDtypeStruct(conv_all.shape, conv_all.dtype),
        jax.ShapeDtypeStruct(ret_all.shape, ret_all.dtype),
    )
    out_specs = (
        pl.BlockSpec((nb, tt, D_MODEL), lambda b, j: (b, j, 0)),
        pl.BlockSpec((None, nb, CONV_W - 1, D_CONV), lambda b, j: (layer, b, 0, 0)),
        pl.BlockSpec((None, nb, RET_HEADS, RET_DK, RET_DV), lambda b, j: (layer, b, 0, 0, 0)),
    )
    scratch_shapes = [
        pltpu.VMEM((nb, RET_HEADS, RET_DK, RET_DV), F32),
        pltpu.VMEM((nb, D_CONV // LANES, tt + SUBLANES, LANES), F32),
        pltpu.VMEM((RET_HEADS, chunk, chunk), F32),
        pltpu.VMEM((RET_HEADS, chunk, RET_DV), F32),
        pltpu.VMEM((RET_HEADS, chunk, RET_DK), F32),
    ]
    return pl.pallas_call(
        functools.partial(_layer_kernel, nb=nb, tt=tt, chunk=chunk, final_norm=final_norm),
        grid=(batch // nb, seq // tt),
        in_specs=in_specs,
        out_specs=out_specs,
        out_shape=out_shape,
        scratch_shapes=scratch_shapes,
        input_output_aliases={len(args) - 2: 1, len(args) - 1: 2},
        compiler_params=pltpu.CompilerParams(
            dimension_semantics=("arbitrary", "arbitrary"),
            vmem_limit_bytes=VMEM_LIMIT_BYTES),
        name=f"mixer_layer_t{seq}",
    )(*args)


def kernel(x_prompt, x_sample, state_conv, state_ret, norm_g, w_in, conv_w, w_a, w_r, w_o, final_g):
    log_gamma = jnp.log(1.0 - 2.0 ** (-5.0 - jnp.arange(RET_HEADS, dtype=F32)))
    bp, tp, _ = x_prompt.shape
    ts = x_sample.shape[1]
    rope_p = _rope_table(tp, 0)
    rope_s = _rope_table(ts, PAST_LEN)
    col = jnp.arange(D_IN)
    gate_cols = ((col >= OFF_ZA) & (col < OFF_Q)) | (col >= OFF_ZR)
    win = (w_in * jnp.where(gate_cols, 0.5, 1.0).astype(F32)).astype(BF16)
    wsq = jnp.stack([0.5 * w_a, 0.5 * w_r, w_o], axis=1).astype(BF16)
    vecs = jnp.concatenate(
        [norm_g[:, None, :], conv_w, jnp.broadcast_to(final_g, (DEPTH, 1, D_MODEL)),
         jnp.zeros((DEPTH, VEC_ROWS - ROW_FINAL_GAIN - 1, D_MODEL), F32)], axis=1)
    conv_p0 = jnp.zeros((1, bp, CONV_W - 1, D_CONV), x_prompt.dtype)
    ret_p0 = jnp.zeros((1, bp, RET_HEADS, RET_DK, RET_DV), F32)

    xp, xs = x_prompt, x_sample
    conv_p = jnp.zeros((DEPTH,) + conv_p0.shape[1:], conv_p0.dtype)
    ret_p = jnp.zeros((DEPTH,) + ret_p0.shape[1:], F32)
    conv_s = jnp.zeros_like(state_conv)
    ret_s = jnp.zeros_like(state_ret)
    for layer in range(DEPTH):
        final_norm = layer == DEPTH - 1
        xp, conv_p, ret_p = _mixer_layer(layer, 0, final_norm, xp, rope_p, conv_p0, ret_p0,
                                         log_gamma, vecs, win, wsq, conv_p, ret_p)
        xs, conv_s, ret_s = _mixer_layer(layer, layer, final_norm, xs, rope_s, state_conv,
                                         state_ret, log_gamma, vecs, win, wsq, conv_s, ret_s)
    return (xp, xs, conv_p, ret_p, conv_s, ret_s)
```

```python
import functools

import jax
import jax.numpy as jnp
from jax import lax
from jax.experimental import pallas as pl
from jax.experimental.pallas import tpu as pltpu

D_MODEL = 1024
DEPTH = 4
PAST_LEN = 1024
D_CONV = D_MODEL
CONV_W = 3
RET_HEADS = 4
RET_DK = D_MODEL // 8
RET_DV = D_MODEL // 4
D_QK = RET_HEADS * RET_DK
D_RV = RET_HEADS * RET_DV
ROPE_BASE = 10000.0
EPS = 1e-6

OFF_HC = 0
OFF_B = OFF_HC + D_CONV
OFF_C = OFF_B + D_CONV
OFF_ZA = OFF_C + D_CONV
OFF_Q = OFF_ZA + D_CONV
OFF_K = OFF_Q + D_QK
OFF_V = OFF_K + D_QK
OFF_ZR = OFF_V + D_RV
OFF_GA = OFF_ZR + D_RV
OFF_GR = OFF_GA + D_MODEL
D_IN = OFF_GR + D_MODEL

SUBLANES = 8
LANES = 128
HALF = RET_DK // 2
MAX_CHUNK = 256
TILE_ROWS = 512
MIN_TILE_ROWS = 256
FIRST_DOT_ROWS = 128
ROPE_ROWS = 512
VMEM_LIMIT_BYTES = 60000 * 1024

ROW_GAIN = 0
ROW_CONV = 1
ROW_FINAL_GAIN = ROW_CONV + CONV_W
VEC_ROWS = SUBLANES
W_A, W_R, W_O = 0, 1, 2

F32 = jnp.float32
BF16 = jnp.bfloat16


def _rope_kernel(rope_ref, *, pos0, rows):
    shape = (rows, RET_DK)
    pos = lax.broadcasted_iota(jnp.int32, shape, 0) + (pl.program_id(0) * rows + pos0)
    lane = lax.broadcasted_iota(jnp.int32, shape, 1)
    idx = jnp.where(lane >= HALF, lane - HALF, lane).astype(F32)
    inv = 1.0 / (ROPE_BASE ** (idx / HALF))
    ang = pos.astype(F32) * inv
    sin = jnp.sin(ang)
    rope_ref[:, :RET_DK] = jnp.cos(ang)
    rope_ref[:, RET_DK:] = jnp.where(lane >= HALF, sin, -sin)


def _rope_table(seq, pos0):
    rows = min(seq, ROPE_ROWS)
    return pl.pallas_call(
        functools.partial(_rope_kernel, pos0=pos0, rows=rows),
        grid=(seq // rows,),
        out_specs=pl.BlockSpec((rows, 2 * RET_DK), lambda i: (i, 0)),
        out_shape=jax.ShapeDtypeStruct((seq, 2 * RET_DK), F32),
        name="rope_table",
    )()


def _twice_sigmoid_of_twice(half_x):
    return jnp.tanh(half_x) + 1.0


def _rms_scale(x, gain):
    return x * lax.rsqrt(jnp.mean(x * x, axis=-1, keepdims=True) + EPS) * gain


def _layer_kernel(lg_ref, x_ref, rope_ref, conv0_ref, ret0_ref, vec_ref, win_ref, wsq_ref,
                  conv_all_ref, ret_all_ref, y_ref, convn_ref, retn_ref,
                  s_ref, ubuf_ref, dec_ref, qdec_ref, kdec_ref, *, nb, tt, chunk, final_norm):
    del conv_all_ref, ret_all_ref
    j = pl.program_id(1)
    nj = pl.num_programs(1)
    m = nb * tt

    @pl.when((pl.program_id(0) == 0) & (j == 0))
    def _init_decay_tables():
        row = lax.broadcasted_iota(jnp.int32, (chunk, chunk), 0)
        col = lax.broadcasted_iota(jnp.int32, (chunk, chunk), 1)
        rel = (row - col).astype(F32)
        rq = lax.broadcasted_iota(jnp.int32, (chunk, RET_DV), 0).astype(F32)
        rk = lax.broadcasted_iota(jnp.int32, (chunk, RET_DK), 0).astype(F32)
        for h in range(RET_HEADS):
            lg = lg_ref[h]
            dec_ref[h] = jnp.where(rel >= 0, jnp.exp(lg * jnp.maximum(rel, 0.0)), 0.0)
            qdec_ref[h] = jnp.exp(lg * (rq + 1.0))
            kdec_ref[h] = jnp.exp(lg * (chunk - 1.0 - rk))

    @pl.when(j == 0)
    def _load_state():
        s_ref[...] = ret0_ref[...]
        for s in range(nb):
            for c in range(D_CONV // LANES):
                lanes = slice(c * LANES, (c + 1) * LANES)
                ubuf_ref[s, c, SUBLANES - 2:SUBLANES, :] = conv0_ref[s, :, lanes]

    xs = x_ref[...].reshape(m, D_MODEL)
    hb = _rms_scale(xs, vec_ref[ROW_GAIN:ROW_GAIN + 1, :]).astype(BF16)

    def proj(off, n):
        return jnp.dot(hb, win_ref[:, off:off + n], preferred_element_type=F32)

    def tap(i):
        return vec_ref[ROW_CONV + i:ROW_CONV + i + 1, :]

    c_proj = jnp.concatenate(
        [jnp.dot(hb[r:r + FIRST_DOT_ROWS], win_ref[:, OFF_C:OFF_C + D_CONV],
                 preferred_element_type=F32) for r in range(0, m, FIRST_DOT_ROWS)], axis=0)
    u = c_proj * proj(OFF_HC, D_CONV)
    gate_a = _twice_sigmoid_of_twice(proj(OFF_GA, D_MODEL))
    q = proj(OFF_Q, D_QK)
    k = proj(OFF_K, D_QK)
    v = proj(OFF_V, D_RV).astype(BF16)
    convs = []
    for s in range(nb):
        us = u[s * tt:(s + 1) * tt]
        for c in range(D_CONV // LANES):
            ubuf_ref[s, c, SUBLANES:SUBLANES + tt, :] = us[:, c * LANES:(c + 1) * LANES]

        def delayed(d):
            return jnp.concatenate(
                [ubuf_ref[s, c, SUBLANES - d:SUBLANES - d + tt, :]
                 for c in range(D_CONV // LANES)], axis=1)

        convs.append(tap(0) * delayed(2) + tap(1) * delayed(1) + tap(2) * us)
    conv = convs[0] if nb == 1 else jnp.concatenate(convs, axis=0)
    half_za = proj(OFF_ZA, D_CONV)
    silu_za = half_za * _twice_sigmoid_of_twice(half_za)
    a_in = ((proj(OFF_B, D_CONV) * conv) * silu_za).astype(BF16)

    ubuf_ref[:, :, SUBLANES - 2:SUBLANES, :] = ubuf_ref[:, :, tt + SUBLANES - 2:tt + SUBLANES, :]

    merged = gate_a * jnp.dot(a_in, wsq_ref[W_A], preferred_element_type=F32)
    cos_t = rope_ref[:, :RET_DK]
    sin_t = rope_ref[:, RET_DK:]
    if nb > 1:
        cos_t = jnp.concatenate([cos_t] * nb, axis=0)
        sin_t = jnp.concatenate([sin_t] * nb, axis=0)

    def rotary(xh):
        return xh * cos_t + pltpu.roll(xh, HALF, 1) * sin_t

    blocks = [(h, s, s * tt + c * chunk) for c in range(tt // chunk) for s in range(nb)
              for h in range(RET_HEADS)]
    qb, kf, scores = {}, {}, {}
    for h in range(RET_HEADS):
        qb[h] = rotary(q[:, h * RET_DK:(h + 1) * RET_DK]).astype(BF16)
        kf[h] = rotary(k[:, h * RET_DK:(h + 1) * RET_DK]) * (RET_DK ** -0.5)
    for h, s, r0 in blocks:
        sc = lax.dot_general(qb[h][r0:r0 + chunk], kf[h][r0:r0 + chunk].astype(BF16),
                             (((1,), (1,)), ((), ())), preferred_element_type=F32)
        scores[h, r0] = (sc * dec_ref[h]).astype(BF16)
    o_blocks, state = {}, {}
    for h, s, r0 in blocks:
        vc = v[r0:r0 + chunk, h * RET_DV:(h + 1) * RET_DV]
        st = state.get((h, s))
        if st is None:
            st = s_ref[s, h]
        o_cross = jnp.dot(qb[h][r0:r0 + chunk], st.astype(BF16),
                          preferred_element_type=F32) * qdec_ref[h]
        o = jnp.dot(scores[h, r0], vc, preferred_element_type=F32) + o_cross
        kd = (kf[h][r0:r0 + chunk] * kdec_ref[h]).astype(BF16)
        g_chunk = jnp.exp(jnp.full((1, RET_DV), lg_ref[h] * chunk, F32))
        state[h, s] = g_chunk * st + lax.dot_general(
            kd, vc, (((0,), (0,)), ((), ())), preferred_element_type=F32)
        o_blocks[h, r0] = o * lax.rsqrt(jnp.mean(o * o, axis=-1, keepdims=True) + EPS)
    for (h, s), st in state.items():
        s_ref[s, h] = st
    row_starts = sorted({r0 for _, _, r0 in blocks})
    o_all = jnp.concatenate(
        [jnp.concatenate([o_blocks[h, r0] for h in range(RET_HEADS)], axis=1)
         for r0 in row_starts], axis=0)
    half_zr = proj(OFF_ZR, D_RV)
    silu_zr = half_zr * _twice_sigmoid_of_twice(half_zr)
    o_gated = (o_all * silu_zr).astype(BF16)
    gate_r = _twice_sigmoid_of_twice(proj(OFF_GR, D_MODEL))
    half_r = jnp.dot(o_gated, wsq_ref[W_R], preferred_element_type=F32)
    merged = (merged + gate_r * half_r).astype(BF16)

    out = xs + jnp.dot(merged, wsq_ref[W_O], preferred_element_type=F32)
    if final_norm:
        out = _rms_scale(out, vec_ref[ROW_FINAL_GAIN:ROW_FINAL_GAIN + 1, :])
    y_ref[...] = out.reshape(nb, tt, D_MODEL)

    @pl.when(j == nj - 1)
    def _store_states():
        retn_ref[...] = s_ref[...]
        for s in range(nb):
            for c in range(D_CONV // LANES):
                lanes = slice(c * LANES, (c + 1) * LANES)
                convn_ref[s, :, lanes] = ubuf_ref[s, c, SUBLANES - 2:SUBLANES, :]


def _tile_plan(seq):
    tt = min(seq, TILE_ROWS)
    nb = max(1, MIN_TILE_ROWS // tt)
    return nb, tt


def _mixer_layer(layer, state_layer, final_norm, x, rope, conv0, ret0, log_gamma, vecs, win, wsq,
                 conv_all, ret_all):
    batch, seq, _ = x.shape
    nb, tt = _tile_plan(seq)
    chunk = min(tt, MAX_CHUNK)

    def layer_spec(shape):
        zeros = (0,) * len(shape)
        return pl.BlockSpec((None,) + shape, lambda b, j: (layer,) + zeros,
                            pipeline_mode=pl.Buffered(1))

    in_specs = [
        pl.BlockSpec(memory_space=pltpu.SMEM),
        pl.BlockSpec((nb, tt, D_MODEL), lambda b, j: (b, j, 0)),
        pl.BlockSpec((tt, 2 * RET_DK), lambda b, j: (j, 0)),
        pl.BlockSpec((None, nb, CONV_W - 1, D_CONV), lambda b, j: (state_layer, b, 0, 0)),
        pl.BlockSpec((None, nb, RET_HEADS, RET_DK, RET_DV),
                     lambda b, j: (state_layer, b, 0, 0, 0)),
        layer_spec((VEC_ROWS, D_MODEL)),
        layer_spec((D_MODEL, D_IN)),
        layer_spec((3, D_MODEL, D_MODEL)),
        pl.BlockSpec(memory_space=pl.ANY),
        pl.BlockSpec(memory_space=pl.ANY),
    ]
    args = (log_gamma, x, rope, conv0, ret0, vecs, win, wsq, conv_all, ret_all)
    out_shape = (
        jax.ShapeDtypeStruct(x.shape, x.dtype),
        jax.ShapeDtypeStruct(conv_all.shape, conv_all.dtype),
        jax.ShapeDtypeStruct(ret_all.shape, ret_all.dtype),
    )
    out_specs = (
        pl.BlockSpec((nb, tt, D_MODEL), lambda b, j: (b, j, 0)),
        pl.BlockSpec((None, nb, CONV_W - 1, D_CONV), lambda b, j: (layer, b, 0, 0)),
        pl.BlockSpec((None, nb, RET_HEADS, RET_DK, RET_DV), lambda b, j: (layer, b, 0, 0, 0)),
    )
    scratch_shapes = [
        pltpu.VMEM((nb, RET_HEADS, RET_DK, RET_DV), F32),
        pltpu.VMEM((nb, D_CONV // LANES, tt + SUBLANES, LANES), F32),
        pltpu.VMEM((RET_HEADS, chunk, chunk), F32),
        pltpu.VMEM((RET_HEADS, chunk, RET_DV), F32),
        pltpu.VMEM((RET_HEADS, chunk, RET_DK), F32),
    ]
    return pl.pallas_call(
        functools.partial(_layer_kernel, nb=nb, tt=tt, chunk=chunk, final_norm=final_norm),
        grid=(batch // nb, seq // tt),
        in_specs=in_specs,
        out_specs=out_specs,
        out_shape=out_shape,
        scratch_shapes=scratch_shapes,
        input_output_aliases={len(args) - 2: 1, len(args) - 1: 2},
        compiler_params=pltpu.CompilerParams(
            dimension_semantics=("arbitrary", "arbitrary"),
            vmem_limit_bytes=VMEM_LIMIT_BYTES),
        name=f"mixer_layer_t{seq}",
    )(*args)


def kernel(x_prompt, x_sample, state_conv, state_ret, norm_g, w_in, conv_w, w_a, w_r, w_o, final_g):
    log_gamma = jnp.log(1.0 - 2.0 ** (-5.0 - jnp.arange(RET_HEADS, dtype=F32)))
    bp, tp, _ = x_prompt.shape
    ts = x_sample.shape[1]
    rope_p = _rope_table(tp, 0)
    rope_s = _rope_table(ts, PAST_LEN)
    col = jnp.arange(D_IN)
    gate_cols = ((col >= OFF_ZA) & (col < OFF_Q)) | (col >= OFF_ZR)
    win = (w_in * jnp.where(gate_cols, 0.5, 1.0).astype(F32)).astype(BF16)
    wsq = jnp.stack([0.5 * w_a, 0.5 * w_r, w_o], axis=1).astype(BF16)
    vecs = jnp.concatenate(
        [norm_g[:, None, :], conv_w, jnp.broadcast_to(final_g, (DEPTH, 1, D_MODEL)),
         jnp.zeros((DEPTH, VEC_ROWS - ROW_FINAL_GAIN - 1, D_MODEL), F32)], axis=1)
    conv_p0 = jnp.zeros((1, bp, CONV_W - 1, D_CONV), x_prompt.dtype)
    ret_p0 = jnp.zeros((1, bp, RET_HEADS, RET_DK, RET_DV), F32)

    xp, xs = x_prompt, x_sample
    conv_p = jnp.zeros((DEPTH,) + conv_p0.shape[1:], conv_p0.dtype)
    ret_p = jnp.zeros((DEPTH,) + ret_p0.shape[1:], F32)
    conv_s = jnp.zeros_like(state_conv)
    ret_s = jnp.zeros_like(state_ret)
    for layer in range(DEPTH):
        final_norm = layer == DEPTH - 1
        xp, conv_p, ret_p = _mixer_layer(layer, 0, final_norm, xp, rope_p, conv_p0, ret_p0,
                                         log_gamma, vecs, win, wsq, conv_p, ret_p)
        xs, conv_s, ret_s = _mixer_layer(layer, layer, final_norm, xs, rope_s, state_conv,
                                         state_ret, log_gamma, vecs, win, wsq, conv_s, ret_s)
    return (xp, xs, conv_p, ret_p, conv_s, ret_s)
```

```python
import functools

import jax
import jax.numpy as jnp
from jax import lax
from jax.experimental import pallas as pl
from jax.experimental.pallas import tpu as pltpu

D_MODEL = 1024
DEPTH = 4
PAST_LEN = 1024
D_CONV = D_MODEL
CONV_W = 3
RET_HEADS = 4
RET_DK = D_MODEL // 8
RET_DV = D_MODEL // 4
D_QK = RET_HEADS * RET_DK
D_RV = RET_HEADS * RET_DV
ROPE_BASE = 10000.0
EPS = 1e-6

OFF_HC = 0
OFF_B = OFF_HC + D_CONV
OFF_C = OFF_B + D_CONV
OFF_ZA = OFF_C + D_CONV
OFF_Q = OFF_ZA + D_CONV
OFF_K = OFF_Q + D_QK
OFF_V = OFF_K + D_QK
OFF_ZR = OFF_V + D_RV
OFF_GA = OFF_ZR + D_RV
OFF_GR = OFF_GA + D_MODEL
D_IN = OFF_GR + D_MODEL

SUBLANES = 8
LANES = 128
HALF = RET_DK // 2
MAX_CHUNK = 256
TILE_ROWS = 512
SUB_TILES = 2
MIN_TILE_ROWS = 256
FIRST_DOT_ROWS = 128
ROPE_ROWS = 512
VMEM_LIMIT_BYTES = 60000 * 1024

ROW_GAIN = 0
ROW_CONV = 1
ROW_FINAL_GAIN = ROW_CONV + CONV_W
VEC_ROWS = SUBLANES
W_A, W_R, W_O = 0, 1, 2

F32 = jnp.float32
BF16 = jnp.bfloat16


def _rope_kernel(rope_ref, *, pos0, rows):
    shape = (rows, RET_DK)
    pos = lax.broadcasted_iota(jnp.int32, shape, 0) + (pl.program_id(0) * rows + pos0)
    lane = lax.broadcasted_iota(jnp.int32, shape, 1)
    idx = jnp.where(lane >= HALF, lane - HALF, lane).astype(F32)
    inv = 1.0 / (ROPE_BASE ** (idx / HALF))
    ang = pos.astype(F32) * inv
    sin = jnp.sin(ang)
    rope_ref[:, :RET_DK] = jnp.cos(ang)
    rope_ref[:, RET_DK:] = jnp.where(lane >= HALF, sin, -sin)


def _rope_table(seq, pos0):
    rows = min(seq, ROPE_ROWS)
    return pl.pallas_call(
        functools.partial(_rope_kernel, pos0=pos0, rows=rows),
        grid=(seq // rows,),
        out_specs=pl.BlockSpec((rows, 2 * RET_DK), lambda i: (i, 0)),
        out_shape=jax.ShapeDtypeStruct((seq, 2 * RET_DK), F32),
        name="rope_table",
    )()


def _twice_sigmoid_of_twice(half_x):
    return jnp.tanh(half_x) + 1.0


def _rms_scale(x, gain):
    return x * lax.rsqrt(jnp.mean(x * x, axis=-1, keepdims=True) + EPS) * gain


def _layer_kernel(*refs, sub, **static):
    if sub == 1:
        _tile_step(0, *refs, sub=sub, **static)
    else:
        def body(tile, carry):
            _tile_step(tile, *refs, sub=sub, **static)
            return carry
        lax.fori_loop(0, sub, body, 0)


def _tile_step(tile, lg_ref, x_ref, rope_ref, conv0_ref, ret0_ref, vec_ref, win_ref, wsq_ref,
               conv_all_ref, ret_all_ref, y_ref, convn_ref, retn_ref,
               s_ref, ubuf_ref, dec_ref, qdec_ref, kdec_ref, *, nb, tt, chunk, final_norm, sub):
    del conv_all_ref, ret_all_ref
    j = pl.program_id(1)
    nj = pl.num_programs(1)
    m = nb * tt
    rows = slice(None) if sub == 1 else pl.ds(pl.multiple_of(tile * tt, tt), tt)
    first_tile = (j == 0) & (tile == 0)
    last_tile = (j == nj - 1) & (tile == sub - 1)

    @pl.when((pl.program_id(0) == 0) & first_tile)
    def _init_decay_tables():
        row = lax.broadcasted_iota(jnp.int32, (chunk, chunk), 0)
        col = lax.broadcasted_iota(jnp.int32, (chunk, chunk), 1)
        rel = (row - col).astype(F32)
        rq = lax.broadcasted_iota(jnp.int32, (chunk, RET_DV), 0).astype(F32)
        rk = lax.broadcasted_iota(jnp.int32, (chunk, RET_DK), 0).astype(F32)
        for h in range(RET_HEADS):
            lg = lg_ref[h]
            dec_ref[h] = jnp.where(rel >= 0, jnp.exp(lg * jnp.maximum(rel, 0.0)), 0.0)
            qdec_ref[h] = jnp.exp(lg * (rq + 1.0))
            kdec_ref[h] = jnp.exp(lg * (chunk - 1.0 - rk))

    @pl.when(first_tile)
    def _load_state():
        s_ref[...] = ret0_ref[...]
        for s in range(nb):
            for c in range(D_CONV // LANES):
                lanes = slice(c * LANES, (c + 1) * LANES)
                ubuf_ref[s, c, SUBLANES - 2:SUBLANES, :] = conv0_ref[s, :, lanes]

    xs = x_ref[:, rows, :].reshape(m, D_MODEL)
    hb = _rms_scale(xs, vec_ref[ROW_GAIN:ROW_GAIN + 1, :]).astype(BF16)

    def proj(off, n):
        return jnp.dot(hb, win_ref[:, off:off + n], preferred_element_type=F32)

    def tap(i):
        return vec_ref[ROW_CONV + i:ROW_CONV + i + 1, :]

    c_proj = jnp.concatenate(
        [jnp.dot(hb[r:r + FIRST_DOT_ROWS], win_ref[:, OFF_C:OFF_C + D_CONV],
                 preferred_element_type=F32) for r in range(0, m, FIRST_DOT_ROWS)], axis=0)
    u = c_proj * proj(OFF_HC, D_CONV)
    gate_a = _twice_sigmoid_of_twice(proj(OFF_GA, D_MODEL))
    q = proj(OFF_Q, D_QK)
    k = proj(OFF_K, D_QK)
    v = proj(OFF_V, D_RV).astype(BF16)
    convs = []
    for s in range(nb):
        us = u[s * tt:(s + 1) * tt]
        for c in range(D_CONV // LANES):
            ubuf_ref[s, c, SUBLANES:SUBLANES + tt, :] = us[:, c * LANES:(c + 1) * LANES]

        def delayed(d):
            return jnp.concatenate(
                [ubuf_ref[s, c, SUBLANES - d:SUBLANES - d + tt, :]
                 for c in range(D_CONV // LANES)], axis=1)

        convs.append(tap(0) * delayed(2) + tap(1) * delayed(1) + tap(2) * us)
    conv = convs[0] if nb == 1 else jnp.concatenate(convs, axis=0)
    half_za = proj(OFF_ZA, D_CONV)
    silu_za = half_za * _twice_sigmoid_of_twice(half_za)
    a_in = ((proj(OFF_B, D_CONV) * conv) * silu_za).astype(BF16)

    ubuf_ref[:, :, SUBLANES - 2:SUBLANES, :] = ubuf_ref[:, :, tt + SUBLANES - 2:tt + SUBLANES, :]

    merged = gate_a * jnp.dot(a_in, wsq_ref[W_A], preferred_element_type=F32)
    cos_t = rope_ref[rows, :RET_DK]
    sin_t = rope_ref[rows, RET_DK:]
    if nb > 1:
        cos_t = jnp.concatenate([cos_t] * nb, axis=0)
        sin_t = jnp.concatenate([sin_t] * nb, axis=0)

    def rotary(xh):
        return xh * cos_t + pltpu.roll(xh, HALF, 1) * sin_t

    blocks = [(h, s, s * tt + c * chunk) for c in range(tt // chunk) for s in range(nb)
              for h in range(RET_HEADS)]
    qb, kf, scores = {}, {}, {}
    for h in range(RET_HEADS):
        qb[h] = rotary(q[:, h * RET_DK:(h + 1) * RET_DK]).astype(BF16)
        kf[h] = rotary(k[:, h * RET_DK:(h + 1) * RET_DK]) * (RET_DK ** -0.5)
    for h, s, r0 in blocks:
        sc = lax.dot_general(qb[h][r0:r0 + chunk], kf[h][r0:r0 + chunk].astype(BF16),
                             (((1,), (1,)), ((), ())), preferred_element_type=F32)
        scores[h, r0] = (sc * dec_ref[h]).astype(BF16)
    o_blocks, state = {}, {}
    for h, s, r0 in blocks:
        vc = v[r0:r0 + chunk, h * RET_DV:(h + 1) * RET_DV]
        st = state.get((h, s))
        if st is None:
            st = s_ref[s, h]
        o_cross = jnp.dot(qb[h][r0:r0 + chunk], st.astype(BF16),
                          preferred_element_type=F32) * qdec_ref[h]
        o = jnp.dot(scores[h, r0], vc, preferred_element_type=F32) + o_cross
        kd = (kf[h][r0:r0 + chunk] * kdec_ref[h]).astype(BF16)
        g_chunk = jnp.exp(jnp.full((1, RET_DV), lg_ref[h] * chunk, F32))
        state[h, s] = g_chunk * st + lax.dot_general(
            kd, vc, (((0,), (0,)), ((), ())), preferred_element_type=F32)
        o_blocks[h, r0] = o * lax.rsqrt(jnp.mean(o * o, axis=-1, keepdims=True) + EPS)
    for (h, s), st in state.items():
        s_ref[s, h] = st
    row_starts = sorted({r0 for _, _, r0 in blocks})
    o_all = jnp.concatenate(
        [jnp.concatenate([o_blocks[h, r0] for h in range(RET_HEADS)], axis=1)
         for r0 in row_starts], axis=0)
    half_zr = proj(OFF_ZR, D_RV)
    silu_zr = half_zr * _twice_sigmoid_of_twice(half_zr)
    o_gated = (o_all * silu_zr).astype(BF16)
    gate_r = _twice_sigmoid_of_twice(proj(OFF_GR, D_MODEL))
    half_r = jnp.dot(o_gated, wsq_ref[W_R], preferred_element_type=F32)
    merged = (merged + gate_r * half_r).astype(BF16)

    out = xs + jnp.dot(merged, wsq_ref[W_O], preferred_element_type=F32)
    if final_norm:
        out = _rms_scale(out, vec_ref[ROW_FINAL_GAIN:ROW_FINAL_GAIN + 1, :])
    y_ref[:, rows, :] = out.reshape(nb, tt, D_MODEL)

    @pl.when(last_tile)
    def _store_states():
        retn_ref[...] = s_ref[...]
        for s in range(nb):
            for c in range(D_CONV // LANES):
                lanes = slice(c * LANES, (c + 1) * LANES)
                convn_ref[s, :, lanes] = ubuf_ref[s, c, SUBLANES - 2:SUBLANES, :]


def _tile_plan(seq):
    tt = min(seq, TILE_ROWS)
    nb = max(1, MIN_TILE_ROWS // tt)
    sub = SUB_TILES if nb == 1 and seq % (tt * SUB_TILES) == 0 else 1
    return nb, tt, sub


def _mixer_layer(layer, state_layer, final_norm, x, rope, conv0, ret0, log_gamma, vecs, win, wsq,
                 conv_all, ret_all):
    batch, seq, _ = x.shape
    nb, tt, sub = _tile_plan(seq)
    chunk = min(tt, MAX_CHUNK)
    step_rows = tt * sub

    def layer_spec(shape):
        zeros = (0,) * len(shape)
        return pl.BlockSpec((None,) + shape, lambda b, j: (layer,) + zeros,
                            pipeline_mode=pl.Buffered(1))

    in_specs = [
        pl.BlockSpec(memory_space=pltpu.SMEM),
        pl.BlockSpec((nb, step_rows, D_MODEL), lambda b, j: (b, j, 0)),
        pl.BlockSpec((step_rows, 2 * RET_DK), lambda b, j: (j, 0)),
        pl.BlockSpec((None, nb, CONV_W - 1, D_CONV), lambda b, j: (state_layer, b, 0, 0)),
        pl.BlockSpec((None, nb, RET_HEADS, RET_DK, RET_DV),
                     lambda b, j: (state_layer, b, 0, 0, 0)),
        layer_spec((VEC_ROWS, D_MODEL)),
        layer_spec((D_MODEL, D_IN)),
        layer_spec((3, D_MODEL, D_MODEL)),
        pl.BlockSpec(memory_space=pl.ANY),
        pl.BlockSpec(memory_space=pl.ANY),
    ]
    args = (log_gamma, x, rope, conv0, ret0, vecs, win, wsq, conv_all, ret_all)
    out_shape = (
        jax.ShapeDtypeStruct(x.shape, x.dtype),
        jax.ShapeDtypeStruct(conv_all.shape, conv_all.dtype),
        jax.ShapeDtypeStruct(ret_all.shape, ret_all.dtype),
    )
    out_specs = (
        pl.BlockSpec((nb, step_rows, D_MODEL), lambda b, j: (b, j, 0)),
        pl.BlockSpec((None, nb, CONV_W - 1, D_CONV), lambda b, j: (layer, b, 0, 0)),
        pl.BlockSpec((None, nb, RET_HEADS, RET_DK, RET_DV), lambda b, j: (layer, b, 0, 0, 0)),
    )
    scratch_shapes = [
        pltpu.VMEM((nb, RET_HEADS, RET_DK, RET_DV), F32),
        pltpu.VMEM((nb, D_CONV // LANES, tt + SUBLANES, LANES), F32),
        pltpu.VMEM((RET_HEADS, chunk, chunk), F32),
        pltpu.VMEM((RET_HEADS, chunk, RET_DV), F32),
        pltpu.VMEM((RET_HEADS, chunk, RET_DK), F32),
    ]
    return pl.pallas_call(
        functools.partial(_layer_kernel, nb=nb, tt=tt, chunk=chunk, final_norm=final_norm,
                          sub=sub),
        grid=(batch // nb, seq // step_rows),
        in_specs=in_specs,
        out_specs=out_specs,
        out_shape=out_shape,
        scratch_shapes=scratch_shapes,
        input_output_aliases={len(args) - 2: 1, len(args) - 1: 2},
        compiler_params=pltpu.CompilerParams(
            dimension_semantics=("arbitrary", "arbitrary"),
            vmem_limit_bytes=VMEM_LIMIT_BYTES),
        name=f"mixer_layer_t{seq}",
    )(*args)


def kernel(x_prompt, x_sample, state_conv, state_ret, norm_g, w_in, conv_w, w_a, w_r, w_o, final_g):
    log_gamma = jnp.log(1.0 - 2.0 ** (-5.0 - jnp.arange(RET_HEADS, dtype=F32)))
    bp, tp, _ = x_prompt.shape
    ts = x_sample.shape[1]
    rope_p = _rope_table(tp, 0)
    rope_s = _rope_table(ts, PAST_LEN)
    col = jnp.arange(D_IN)
    gate_cols = ((col >= OFF_ZA) & (col < OFF_Q)) | (col >= OFF_ZR)
    win = (w_in * jnp.where(gate_cols, 0.5, 1.0).astype(F32)).astype(BF16)
    wsq = jnp.stack([0.5 * w_a, 0.5 * w_r, w_o], axis=1).astype(BF16)
    vecs = jnp.concatenate(
        [norm_g[:, None, :], conv_w, jnp.broadcast_to(final_g, (DEPTH, 1, D_MODEL)),
         jnp.zeros((DEPTH, VEC_ROWS - ROW_FINAL_GAIN - 1, D_MODEL), F32)], axis=1)
    conv_p0 = jnp.zeros((1, bp, CONV_W - 1, D_CONV), x_prompt.dtype)
    ret_p0 = jnp.zeros((1, bp, RET_HEADS, RET_DK, RET_DV), F32)

    xp, xs = x_prompt, x_sample
    conv_p = jnp.zeros((DEPTH,) + conv_p0.shape[1:], conv_p0.dtype)
    ret_p = jnp.zeros((DEPTH,) + ret_p0.shape[1:], F32)
    conv_s = jnp.zeros_like(state_conv)
    ret_s = jnp.zeros_like(state_ret)
    for layer in range(DEPTH):
        final_norm = layer == DEPTH - 1
        xp, conv_p, ret_p = _mixer_layer(layer, 0, final_norm, xp, rope_p, conv_p0, ret_p0,
                                         log_gamma, vecs, win, wsq, conv_p, ret_p)
        xs, conv_s, ret_s = _mixer_layer(layer, layer, final_norm, xs, rope_s, state_conv,
                                         state_ret, log_gamma, vecs, win, wsq, conv_s, ret_s)
    return (xp, xs, conv_p, ret_p, conv_s, ret_s)
```

```python
import functools

import jax
import jax.numpy as jnp
from jax import lax
from jax.experimental import pallas as pl
from jax.experimental.pallas import tpu as pltpu

D_MODEL = 1024
DEPTH = 4
PAST_LEN = 1024
D_CONV = D_MODEL
CONV_W = 3
RET_HEADS = 4
RET_DK = D_MODEL // 8
RET_DV = D_MODEL // 4
D_QK = RET_HEADS * RET_DK
D_RV = RET_HEADS * RET_DV
ROPE_BASE = 10000.0
EPS = 1e-6

OFF_HC = 0
OFF_B = OFF_HC + D_CONV
OFF_C = OFF_B + D_CONV
OFF_ZA = OFF_C + D_CONV
OFF_Q = OFF_ZA + D_CONV
OFF_K = OFF_Q + D_QK
OFF_V = OFF_K + D_QK
OFF_ZR = OFF_V + D_RV
OFF_GA = OFF_ZR + D_RV
OFF_GR = OFF_GA + D_MODEL
D_IN = OFF_GR + D_MODEL

SUBLANES = 8
LANES = 128
HALF = RET_DK // 2
MAX_CHUNK = 256
TILE_ROWS = 512
MIN_TILE_ROWS = 256
FIRST_DOT_ROWS = 128
ROPE_ROWS = 512
VMEM_LIMIT_BYTES = 60000 * 1024

ROW_GAIN = 0
ROW_CONV = 1
ROW_FINAL_GAIN = ROW_CONV + CONV_W
VEC_ROWS = SUBLANES
W_A, W_R, W_O = 0, 1, 2

F32 = jnp.float32
BF16 = jnp.bfloat16


def _rope_kernel(rope_ref, *, pos0, rows):
    shape = (rows, RET_DK)
    pos = lax.broadcasted_iota(jnp.int32, shape, 0) + (pl.program_id(0) * rows + pos0)
    lane = lax.broadcasted_iota(jnp.int32, shape, 1)
    idx = jnp.where(lane >= HALF, lane - HALF, lane).astype(F32)
    inv = 1.0 / (ROPE_BASE ** (idx / HALF))
    ang = pos.astype(F32) * inv
    sin = jnp.sin(ang)
    rope_ref[:, :RET_DK] = jnp.cos(ang)
    rope_ref[:, RET_DK:] = jnp.where(lane >= HALF, sin, -sin)


def _rope_table(seq, pos0):
    rows = min(seq, ROPE_ROWS)
    return pl.pallas_call(
        functools.partial(_rope_kernel, pos0=pos0, rows=rows),
        grid=(seq // rows,),
        out_specs=pl.BlockSpec((rows, 2 * RET_DK), lambda i: (i, 0)),
        out_shape=jax.ShapeDtypeStruct((seq, 2 * RET_DK), F32),
        name="rope_table",
    )()


def _twice_sigmoid_of_twice(half_x):
    return jnp.tanh(half_x) + 1.0


def _rms_scale(x, gain):
    return x * lax.rsqrt(jnp.mean(x * x, axis=-1, keepdims=True) + EPS) * gain


def _decay_kernel(lg_ref, dec_ref, qdec_ref, kdec_ref, *, chunk):
    row = lax.broadcasted_iota(jnp.int32, (chunk, chunk), 0)
    col = lax.broadcasted_iota(jnp.int32, (chunk, chunk), 1)
    rel = (row - col).astype(F32)
    rq = lax.broadcasted_iota(jnp.int32, (chunk, RET_DV), 0).astype(F32)
    rk = lax.broadcasted_iota(jnp.int32, (chunk, RET_DK), 0).astype(F32)
    for h in range(RET_HEADS):
        lg = lg_ref[h]
        dec_ref[h] = jnp.where(rel >= 0, jnp.exp(lg * jnp.maximum(rel, 0.0)), 0.0)
        qdec_ref[h] = jnp.exp(lg * (rq + 1.0))
        kdec_ref[h] = jnp.exp(lg * (chunk - 1.0 - rk))


def _decay_tables(log_gamma, chunk):
    shapes = ((chunk, chunk), (chunk, RET_DV), (chunk, RET_DK))
    return pl.pallas_call(
        functools.partial(_decay_kernel, chunk=chunk),
        in_specs=[pl.BlockSpec(memory_space=pltpu.SMEM)],
        out_shape=tuple(jax.ShapeDtypeStruct((RET_HEADS,) + s, F32) for s in shapes),
        name="decay_tables",
    )(log_gamma)


def _layer_kernel(x_ref, rope_ref, conv0_ref, ret0_ref, vec_ref, win_ref, wsq_ref,
                  dec_ref, qdec_ref, kdec_ref, conv_all_ref, ret_all_ref,
                  y_ref, convn_ref, retn_ref, s_ref, ubuf_ref, *, nb, tt, chunk, final_norm):
    del conv_all_ref, ret_all_ref
    j = pl.program_id(1)
    nj = pl.num_programs(1)
    m = nb * tt

    @pl.when(j == 0)
    def _load_state():
        s_ref[...] = ret0_ref[...]
        for s in range(nb):
            for c in range(D_CONV // LANES):
                lanes = slice(c * LANES, (c + 1) * LANES)
                ubuf_ref[s, c, SUBLANES - 2:SUBLANES, :] = conv0_ref[s, :, lanes]

    xs = x_ref[...].reshape(m, D_MODEL)
    hb = _rms_scale(xs, vec_ref[ROW_GAIN:ROW_GAIN + 1, :]).astype(BF16)

    def proj(off, n):
        return jnp.dot(hb, win_ref[:, off:off + n], preferred_element_type=F32)

    def tap(i):
        return vec_ref[ROW_CONV + i:ROW_CONV + i + 1, :]

    c_proj = jnp.concatenate(
        [jnp.dot(hb[r:r + FIRST_DOT_ROWS], win_ref[:, OFF_C:OFF_C + D_CONV],
                 preferred_element_type=F32) for r in range(0, m, FIRST_DOT_ROWS)], axis=0)
    u = c_proj * proj(OFF_HC, D_CONV)
    gate_a = _twice_sigmoid_of_twice(proj(OFF_GA, D_MODEL))
    q = proj(OFF_Q, D_QK)
    k = proj(OFF_K, D_QK)
    v = proj(OFF_V, D_RV).astype(BF16)
    convs = []
    for s in range(nb):
        us = u[s * tt:(s + 1) * tt]
        for c in range(D_CONV // LANES):
            ubuf_ref[s, c, SUBLANES:SUBLANES + tt, :] = us[:, c * LANES:(c + 1) * LANES]

        def delayed(d):
            return jnp.concatenate(
                [ubuf_ref[s, c, SUBLANES - d:SUBLANES - d + tt, :]
                 for c in range(D_CONV // LANES)], axis=1)

        convs.append(tap(0) * delayed(2) + tap(1) * delayed(1) + tap(2) * us)
    conv = convs[0] if nb == 1 else jnp.concatenate(convs, axis=0)
    half_za = proj(OFF_ZA, D_CONV)
    silu_za = half_za * _twice_sigmoid_of_twice(half_za)
    a_in = ((proj(OFF_B, D_CONV) * conv) * silu_za).astype(BF16)

    ubuf_ref[:, :, SUBLANES - 2:SUBLANES, :] = ubuf_ref[:, :, tt + SUBLANES - 2:tt + SUBLANES, :]

    merged = gate_a * jnp.dot(a_in, wsq_ref[W_A], preferred_element_type=F32)
    cos_t = rope_ref[:, :RET_DK]
    sin_t = rope_ref[:, RET_DK:]
    if nb > 1:
        cos_t = jnp.concatenate([cos_t] * nb, axis=0)
        sin_t = jnp.concatenate([sin_t] * nb, axis=0)

    def rotary(xh):
        return xh * cos_t + pltpu.roll(xh, HALF, 1) * sin_t

    blocks = [(h, s, s * tt + c * chunk) for c in range(tt // chunk) for s in range(nb)
              for h in range(RET_HEADS)]
    qb, kf, scores = {}, {}, {}
    for h in range(RET_HEADS):
        qb[h] = rotary(q[:, h * RET_DK:(h + 1) * RET_DK]).astype(BF16)
        kf[h] = rotary(k[:, h * RET_DK:(h + 1) * RET_DK]) * (RET_DK ** -0.5)
    for h, s, r0 in blocks:
        sc = lax.dot_general(qb[h][r0:r0 + chunk], kf[h][r0:r0 + chunk].astype(BF16),
                             (((1,), (1,)), ((), ())), preferred_element_type=F32)
        scores[h, r0] = (sc * dec_ref[h]).astype(BF16)
    o_blocks, state = {}, {}
    for h, s, r0 in blocks:
        vc = v[r0:r0 + chunk, h * RET_DV:(h + 1) * RET_DV]
        st = state.get((h, s))
        if st is None:
            st = s_ref[s, h]
        o_cross = jnp.dot(qb[h][r0:r0 + chunk], st.astype(BF16),
                          preferred_element_type=F32) * qdec_ref[h]
        o = jnp.dot(scores[h, r0], vc, preferred_element_type=F32) + o_cross
        kd = (kf[h][r0:r0 + chunk] * kdec_ref[h]).astype(BF16)
        g_chunk = qdec_ref[h, chunk - 1:chunk, :]
        state[h, s] = g_chunk * st + lax.dot_general(
            kd, vc, (((0,), (0,)), ((), ())), preferred_element_type=F32)
        o_blocks[h, r0] = o * lax.rsqrt(jnp.mean(o * o, axis=-1, keepdims=True) + EPS)
    for (h, s), st in state.items():
        s_ref[s, h] = st
    row_starts = sorted({r0 for _, _, r0 in blocks})
    o_all = jnp.concatenate(
        [jnp.concatenate([o_blocks[h, r0] for h in range(RET_HEADS)], axis=1)
         for r0 in row_starts], axis=0)
    half_zr = proj(OFF_ZR, D_RV)
    silu_zr = half_zr * _twice_sigmoid_of_twice(half_zr)
    o_gated = (o_all * silu_zr).astype(BF16)
    gate_r = _twice_sigmoid_of_twice(proj(OFF_GR, D_MODEL))
    half_r = jnp.dot(o_gated, wsq_ref[W_R], preferred_element_type=F32)
    merged = (merged + gate_r * half_r).astype(BF16)

    out = xs + jnp.dot(merged, wsq_ref[W_O], preferred_element_type=F32)
    if final_norm:
        out = _rms_scale(out, vec_ref[ROW_FINAL_GAIN:ROW_FINAL_GAIN + 1, :])
    y_ref[...] = out.reshape(nb, tt, D_MODEL)

    @pl.when(j == nj - 1)
    def _store_states():
        retn_ref[...] = s_ref[...]
        for s in range(nb):
            for c in range(D_CONV // LANES):
                lanes = slice(c * LANES, (c + 1) * LANES)
                convn_ref[s, :, lanes] = ubuf_ref[s, c, SUBLANES - 2:SUBLANES, :]


def _tile_plan(seq):
    tt = min(seq, TILE_ROWS)
    nb = max(1, MIN_TILE_ROWS // tt)
    return nb, tt


def _mixer_layer(layer, state_layer, final_norm, x, rope, conv0, ret0, decay, vecs, win, wsq,
                 conv_all, ret_all):
    batch, seq, _ = x.shape
    nb, tt = _tile_plan(seq)
    chunk = min(tt, MAX_CHUNK)

    def layer_spec(shape):
        zeros = (0,) * len(shape)
        return pl.BlockSpec((None,) + shape, lambda b, j: (layer,) + zeros,
                            pipeline_mode=pl.Buffered(1))

    def const_spec(shape):
        zeros = (0,) * len(shape)
        return pl.BlockSpec(shape, lambda b, j: zeros, pipeline_mode=pl.Buffered(1))

    in_specs = [
        pl.BlockSpec((nb, tt, D_MODEL), lambda b, j: (b, j, 0)),
        pl.BlockSpec((tt, 2 * RET_DK), lambda b, j: (j, 0)),
        pl.BlockSpec((None, nb, CONV_W - 1, D_CONV), lambda b, j: (state_layer, b, 0, 0)),
        pl.BlockSpec((None, nb, RET_HEADS, RET_DK, RET_DV),
                     lambda b, j: (state_layer, b, 0, 0, 0)),
        layer_spec((VEC_ROWS, D_MODEL)),
        layer_spec((D_MODEL, D_IN)),
        layer_spec((3, D_MODEL, D_MODEL)),
    ] + [const_spec(t.shape) for t in decay] + [
        pl.BlockSpec(memory_space=pl.ANY),
        pl.BlockSpec(memory_space=pl.ANY),
    ]
    args = (x, rope, conv0, ret0, vecs, win, wsq, *decay, conv_all, ret_all)
    out_shape = (
        jax.ShapeDtypeStruct(x.shape, x.dtype),
        jax.ShapeDtypeStruct(conv_all.shape, conv_all.dtype),
        jax.ShapeDtypeStruct(ret_all.shape, ret_all.dtype),
    )
    out_specs = (
        pl.BlockSpec((nb, tt, D_MODEL), lambda b, j: (b, j, 0)),
        pl.BlockSpec((None, nb, CONV_W - 1, D_CONV), lambda b, j: (layer, b, 0, 0)),
        pl.BlockSpec((None, nb, RET_HEADS, RET_DK, RET_DV), lambda b, j: (layer, b, 0, 0, 0)),
    )
    scratch_shapes = [
        pltpu.VMEM((nb, RET_HEADS, RET_DK, RET_DV), F32),
        pltpu.VMEM((nb, D_CONV // LANES, tt + SUBLANES, LANES), F32),
    ]
    return pl.pallas_call(
        functools.partial(_layer_kernel, nb=nb, tt=tt, chunk=chunk, final_norm=final_norm),
        grid=(batch // nb, seq // tt),
        in_specs=in_specs,
        out_specs=out_specs,
        out_shape=out_shape,
        scratch_shapes=scratch_shapes,
        input_output_aliases={len(args) - 2: 1, len(args) - 1: 2},
        compiler_params=pltpu.CompilerParams(
            dimension_semantics=("arbitrary", "arbitrary"),
            vmem_limit_bytes=VMEM_LIMIT_BYTES),
        name=f"mixer_layer_t{seq}",
    )(*args)


def kernel(x_prompt, x_sample, state_conv, state_ret, norm_g, w_in, conv_w, w_a, w_r, w_o, final_g):
    log_gamma = jnp.log(1.0 - 2.0 ** (-5.0 - jnp.arange(RET_HEADS, dtype=F32)))
    bp, tp, _ = x_prompt.shape
    ts = x_sample.shape[1]
    rope_p = _rope_table(tp, 0)
    rope_s = _rope_table(ts, PAST_LEN)
    col = jnp.arange(D_IN)
    gate_cols = ((col >= OFF_ZA) & (col < OFF_Q)) | (col >= OFF_ZR)
    win = (w_in * jnp.where(gate_cols, 0.5, 1.0).astype(F32)).astype(BF16)
    wsq = jnp.stack([0.5 * w_a, 0.5 * w_r, w_o], axis=1).astype(BF16)
    vecs = jnp.concatenate(
        [norm_g[:, None, :], conv_w, jnp.broadcast_to(final_g, (DEPTH, 1, D_MODEL)),
         jnp.zeros((DEPTH, VEC_ROWS - ROW_FINAL_GAIN - 1, D_MODEL), F32)], axis=1)
    conv_p0 = jnp.zeros((1, bp, CONV_W - 1, D_CONV), x_prompt.dtype)
    ret_p0 = jnp.zeros((1, bp, RET_HEADS, RET_DK, RET_DV), F32)

    xp, xs = x_prompt, x_sample
    conv_p = jnp.zeros((DEPTH,) + conv_p0.shape[1:], conv_p0.dtype)
    ret_p = jnp.zeros((DEPTH,) + ret_p0.shape[1:], F32)
    conv_s = jnp.zeros_like(state_conv)
    ret_s = jnp.zeros_like(state_ret)
    decay_p = _decay_tables(log_gamma, min(_tile_plan(tp)[1], MAX_CHUNK))
    decay_s = _decay_tables(log_gamma, min(_tile_plan(ts)[1], MAX_CHUNK))
    for layer in range(DEPTH):
        final_norm = layer == DEPTH - 1
        xp, conv_p, ret_p = _mixer_layer(layer, 0, final_norm, xp, rope_p, conv_p0, ret_p0,
                                         decay_p, vecs, win, wsq, conv_p, ret_p)
        xs, conv_s, ret_s = _mixer_layer(layer, layer, final_norm, xs, rope_s, state_conv,
                                         state_ret, decay_s, vecs, win, wsq, conv_s, ret_s)
    return (xp, xs, conv_p, ret_p, conv_s, ret_s)
```

```python
import functools

import jax
import jax.numpy as jnp
from jax import lax
from jax.experimental import pallas as pl
from jax.experimental.pallas import tpu as pltpu

D_MODEL = 1024
DEPTH = 4
PAST_LEN = 1024
D_CONV = D_MODEL
CONV_W = 3
RET_HEADS = 4
RET_DK = D_MODEL // 8
RET_DV = D_MODEL // 4
D_QK = RET_HEADS * RET_DK
D_RV = RET_HEADS * RET_DV
ROPE_BASE = 10000.0
EPS = 1e-6

OFF_HC = 0
OFF_B = OFF_HC + D_CONV
OFF_C = OFF_B + D_CONV
OFF_ZA = OFF_C + D_CONV
OFF_Q = OFF_ZA + D_CONV
OFF_K = OFF_Q + D_QK
OFF_V = OFF_K + D_QK
OFF_ZR = OFF_V + D_RV
OFF_GA = OFF_ZR + D_RV
OFF_GR = OFF_GA + D_MODEL
D_IN = OFF_GR + D_MODEL

SUBLANES = 8
LANES = 128
HALF = RET_DK // 2
MAX_CHUNK = 256
TILE_ROWS = 256
LONG_TILE_SEQS = 2
MIN_TILE_ROWS = 256
FIRST_DOT_ROWS = 128
ROPE_ROWS = 512
VMEM_LIMIT_BYTES = 60000 * 1024

ROW_GAIN = 0
ROW_CONV = 1
ROW_FINAL_GAIN = ROW_CONV + CONV_W
VEC_ROWS = SUBLANES
W_A, W_R, W_O = 0, 1, 2

F32 = jnp.float32
BF16 = jnp.bfloat16


def _rope_kernel(rope_ref, *, pos0, rows):
    shape = (rows, RET_DK)
    pos = lax.broadcasted_iota(jnp.int32, shape, 0) + (pl.program_id(0) * rows + pos0)
    lane = lax.broadcasted_iota(jnp.int32, shape, 1)
    idx = jnp.where(lane >= HALF, lane - HALF, lane).astype(F32)
    inv = 1.0 / (ROPE_BASE ** (idx / HALF))
    ang = pos.astype(F32) * inv
    sin = jnp.sin(ang)
    rope_ref[:, :RET_DK] = jnp.cos(ang)
    rope_ref[:, RET_DK:] = jnp.where(lane >= HALF, sin, -sin)


def _rope_table(seq, pos0):
    rows = min(seq, ROPE_ROWS)
    return pl.pallas_call(
        functools.partial(_rope_kernel, pos0=pos0, rows=rows),
        grid=(seq // rows,),
        out_specs=pl.BlockSpec((rows, 2 * RET_DK), lambda i: (i, 0)),
        out_shape=jax.ShapeDtypeStruct((seq, 2 * RET_DK), F32),
        name="rope_table",
    )()


def _twice_sigmoid_of_twice(half_x):
    return jnp.tanh(half_x) + 1.0


def _rms_scale(x, gain):
    return x * lax.rsqrt(jnp.mean(x * x, axis=-1, keepdims=True) + EPS) * gain


def _layer_kernel(lg_ref, x_ref, rope_ref, conv0_ref, ret0_ref, vec_ref, win_ref, wsq_ref,
                  conv_all_ref, ret_all_ref, y_ref, convn_ref, retn_ref,
                  s_ref, ubuf_ref, dec_ref, qdec_ref, kdec_ref, *, nb, tt, chunk, final_norm):
    del conv_all_ref, ret_all_ref
    j = pl.program_id(1)
    nj = pl.num_programs(1)
    m = nb * tt

    @pl.when((pl.program_id(0) == 0) & (j == 0))
    def _init_decay_tables():
        row = lax.broadcasted_iota(jnp.int32, (chunk, chunk), 0)
        col = lax.broadcasted_iota(jnp.int32, (chunk, chunk), 1)
        rel = (row - col).astype(F32)
        rq = lax.broadcasted_iota(jnp.int32, (chunk, RET_DV), 0).astype(F32)
        rk = lax.broadcasted_iota(jnp.int32, (chunk, RET_DK), 0).astype(F32)
        for h in range(RET_HEADS):
            lg = lg_ref[h]
            dec_ref[h] = jnp.where(rel >= 0, jnp.exp(lg * jnp.maximum(rel, 0.0)), 0.0)
            qdec_ref[h] = jnp.exp(lg * (rq + 1.0))
            kdec_ref[h] = jnp.exp(lg * (chunk - 1.0 - rk))

    @pl.when(j == 0)
    def _load_state():
        s_ref[...] = ret0_ref[...]
        for s in range(nb):
            for c in range(D_CONV // LANES):
                lanes = slice(c * LANES, (c + 1) * LANES)
                ubuf_ref[s, c, SUBLANES - 2:SUBLANES, :] = conv0_ref[s, :, lanes]

    xs = x_ref[...].reshape(m, D_MODEL)
    hb = _rms_scale(xs, vec_ref[ROW_GAIN:ROW_GAIN + 1, :]).astype(BF16)

    def proj(off, n):
        return jnp.dot(hb, win_ref[:, off:off + n], preferred_element_type=F32)

    def tap(i):
        return vec_ref[ROW_CONV + i:ROW_CONV + i + 1, :]

    c_proj = jnp.concatenate(
        [jnp.dot(hb[r:r + FIRST_DOT_ROWS], win_ref[:, OFF_C:OFF_C + D_CONV],
                 preferred_element_type=F32) for r in range(0, m, FIRST_DOT_ROWS)], axis=0)
    u = c_proj * proj(OFF_HC, D_CONV)
    gate_a = _twice_sigmoid_of_twice(proj(OFF_GA, D_MODEL))
    q = proj(OFF_Q, D_QK)
    k = proj(OFF_K, D_QK)
    v = proj(OFF_V, D_RV).astype(BF16)
    convs = []
    for s in range(nb):
        us = u[s * tt:(s + 1) * tt]
        for c in range(D_CONV // LANES):
            ubuf_ref[s, c, SUBLANES:SUBLANES + tt, :] = us[:, c * LANES:(c + 1) * LANES]

        def delayed(d):
            return jnp.concatenate(
                [ubuf_ref[s, c, SUBLANES - d:SUBLANES - d + tt, :]
                 for c in range(D_CONV // LANES)], axis=1)

        convs.append(tap(0) * delayed(2) + tap(1) * delayed(1) + tap(2) * us)
    conv = convs[0] if nb == 1 else jnp.concatenate(convs, axis=0)
    half_za = proj(OFF_ZA, D_CONV)
    silu_za = half_za * _twice_sigmoid_of_twice(half_za)
    a_in = ((proj(OFF_B, D_CONV) * conv) * silu_za).astype(BF16)

    ubuf_ref[:, :, SUBLANES - 2:SUBLANES, :] = ubuf_ref[:, :, tt + SUBLANES - 2:tt + SUBLANES, :]

    merged = gate_a * jnp.dot(a_in, wsq_ref[W_A], preferred_element_type=F32)
    cos_t = rope_ref[:, :RET_DK]
    sin_t = rope_ref[:, RET_DK:]
    if nb > 1:
        cos_t = jnp.concatenate([cos_t] * nb, axis=0)
        sin_t = jnp.concatenate([sin_t] * nb, axis=0)

    def rotary(xh):
        return xh * cos_t + pltpu.roll(xh, HALF, 1) * sin_t

    blocks = [(h, s, s * tt + c * chunk) for c in range(tt // chunk) for s in range(nb)
              for h in range(RET_HEADS)]
    qb, kf, scores = {}, {}, {}
    for h in range(RET_HEADS):
        qb[h] = rotary(q[:, h * RET_DK:(h + 1) * RET_DK]).astype(BF16)
        kf[h] = rotary(k[:, h * RET_DK:(h + 1) * RET_DK]) * (RET_DK ** -0.5)
    for h, s, r0 in blocks:
        sc = lax.dot_general(qb[h][r0:r0 + chunk], kf[h][r0:r0 + chunk].astype(BF16),
                             (((1,), (1,)), ((), ())), preferred_element_type=F32)
        scores[h, r0] = (sc * dec_ref[h]).astype(BF16)
    o_blocks, state = {}, {}
    for h, s, r0 in blocks:
        vc = v[r0:r0 + chunk, h * RET_DV:(h + 1) * RET_DV]
        st = state.get((h, s))
        if st is None:
            st = s_ref[s, h]
        o_cross = jnp.dot(qb[h][r0:r0 + chunk], st.astype(BF16),
                          preferred_element_type=F32) * qdec_ref[h]
        o = jnp.dot(scores[h, r0], vc, preferred_element_type=F32) + o_cross
        kd = (kf[h][r0:r0 + chunk] * kdec_ref[h]).astype(BF16)
        g_chunk = jnp.exp(jnp.full((1, RET_DV), lg_ref[h] * chunk, F32))
        state[h, s] = g_chunk * st + lax.dot_general(
            kd, vc, (((0,), (0,)), ((), ())), preferred_element_type=F32)
        o_blocks[h, r0] = o * lax.rsqrt(jnp.mean(o * o, axis=-1, keepdims=True) + EPS)
    for (h, s), st in state.items():
        s_ref[s, h] = st
    row_starts = sorted({r0 for _, _, r0 in blocks})
    o_all = jnp.concatenate(
        [jnp.concatenate([o_blocks[h, r0] for h in range(RET_HEADS)], axis=1)
         for r0 in row_starts], axis=0)
    half_zr = proj(OFF_ZR, D_RV)
    silu_zr = half_zr * _twice_sigmoid_of_twice(half_zr)
    o_gated = (o_all * silu_zr).astype(BF16)
    gate_r = _twice_sigmoid_of_twice(proj(OFF_GR, D_MODEL))
    half_r = jnp.dot(o_gated, wsq_ref[W_R], preferred_element_type=F32)
    merged = (merged + gate_r * half_r).astype(BF16)

    out = xs + jnp.dot(merged, wsq_ref[W_O], preferred_element_type=F32)
    if final_norm:
        out = _rms_scale(out, vec_ref[ROW_FINAL_GAIN:ROW_FINAL_GAIN + 1, :])
    y_ref[...] = out.reshape(nb, tt, D_MODEL)

    @pl.when(j == nj - 1)
    def _store_states():
        retn_ref[...] = s_ref[...]
        for s in range(nb):
            for c in range(D_CONV // LANES):
                lanes = slice(c * LANES, (c + 1) * LANES)
                convn_ref[s, :, lanes] = ubuf_ref[s, c, SUBLANES - 2:SUBLANES, :]


def _tile_plan(seq):
    tt = min(seq, TILE_ROWS)
    nb = LONG_TILE_SEQS if seq > tt else max(1, MIN_TILE_ROWS // tt)
    return nb, tt


def _mixer_layer(layer, state_layer, final_norm, x, rope, conv0, ret0, log_gamma, vecs, win, wsq,
                 conv_all, ret_all):
    batch, seq, _ = x.shape
    nb, tt = _tile_plan(seq)
    chunk = min(tt, MAX_CHUNK)

    def layer_spec(shape):
        zeros = (0,) * len(shape)
        return pl.BlockSpec((None,) + shape, lambda b, j: (layer,) + zeros,
                            pipeline_mode=pl.Buffered(1))

    in_specs = [
        pl.BlockSpec(memory_space=pltpu.SMEM),
        pl.BlockSpec((nb, tt, D_MODEL), lambda b, j: (b, j, 0)),
        pl.BlockSpec((tt, 2 * RET_DK), lambda b, j: (j, 0)),
        pl.BlockSpec((None, nb, CONV_W - 1, D_CONV), lambda b, j: (state_layer, b, 0, 0)),
        pl.BlockSpec((None, nb, RET_HEADS, RET_DK, RET_DV),
                     lambda b, j: (state_layer, b, 0, 0, 0)),
        layer_spec((VEC_ROWS, D_MODEL)),
        layer_spec((D_MODEL, D_IN)),
        layer_spec((3, D_MODEL, D_MODEL)),
        pl.BlockSpec(memory_space=pl.ANY),
        pl.BlockSpec(memory_space=pl.ANY),
    ]
    args = (log_gamma, x, rope, conv0, ret0, vecs, win, wsq, conv_all, ret_all)
    out_shape = (
        jax.ShapeDtypeStruct(x.shape, x.dtype),
        jax.ShapeDtypeStruct(conv_all.shape, conv_all.dtype),
        jax.ShapeDtypeStruct(ret_all.shape, ret_all.dtype),
    )
    out_specs = (
        pl.BlockSpec((nb, tt, D_MODEL), lambda b, j: (b, j, 0)),
        pl.BlockSpec((None, nb, CONV_W - 1, D_CONV), lambda b, j: (layer, b, 0, 0)),
        pl.BlockSpec((None, nb, RET_HEADS, RET_DK, RET_DV), lambda b, j: (layer, b, 0, 0, 0)),
    )
    scratch_shapes = [
        pltpu.VMEM((nb, RET_HEADS, RET_DK, RET_DV), F32),
        pltpu.VMEM((nb, D_CONV // LANES, tt + SUBLANES, LANES), F32),
        pltpu.VMEM((RET_HEADS, chunk, chunk), F32),
        pltpu.VMEM((RET_HEADS, chunk, RET_DV), F32),
        pltpu.VMEM((RET_HEADS, chunk, RET_DK), F32),
    ]
    return pl.pallas_call(
        functools.partial(_layer_kernel, nb=nb, tt=tt, chunk=chunk, final_norm=final_norm),
        grid=(batch // nb, seq // tt),
        in_specs=in_specs,
        out_specs=out_specs,
        out_shape=out_shape,
        scratch_shapes=scratch_shapes,
        input_output_aliases={len(args) - 2: 1, len(args) - 1: 2},
        compiler_params=pltpu.CompilerParams(
            dimension_semantics=("arbitrary", "arbitrary"),
            vmem_limit_bytes=VMEM_LIMIT_BYTES),
        name=f"mixer_layer_t{seq}",
    )(*args)


def kernel(x_prompt, x_sample, state_conv, state_ret, norm_g, w_in, conv_w, w_a, w_r, w_o, final_g):
    log_gamma = jnp.log(1.0 - 2.0 ** (-5.0 - jnp.arange(RET_HEADS, dtype=F32)))
    bp, tp, _ = x_prompt.shape
    ts = x_sample.shape[1]
    rope_p = _rope_table(tp, 0)
    rope_s = _rope_table(ts, PAST_LEN)
    col = jnp.arange(D_IN)
    gate_cols = ((col >= OFF_ZA) & (col < OFF_Q)) | (col >= OFF_ZR)
    win = (w_in * jnp.where(gate_cols, 0.5, 1.0).astype(F32)).astype(BF16)
    wsq = jnp.stack([0.5 * w_a, 0.5 * w_r, w_o], axis=1).astype(BF16)
    vecs = jnp.concatenate(
        [norm_g[:, None, :], conv_w, jnp.broadcast_to(final_g, (DEPTH, 1, D_MODEL)),
         jnp.zeros((DEPTH, VEC_ROWS - ROW_FINAL_GAIN - 1, D_MODEL), F32)], axis=1)
    conv_p0 = jnp.zeros((1, bp, CONV_W - 1, D_CONV), x_prompt.dtype)
    ret_p0 = jnp.zeros((1, bp, RET_HEADS, RET_DK, RET_DV), F32)

    xp, xs = x_prompt, x_sample
    conv_p = jnp.zeros((DEPTH,) + conv_p0.shape[1:], conv_p0.dtype)
    ret_p = jnp.zeros((DEPTH,) + ret_p0.shape[1:], F32)
    conv_s = jnp.zeros_like(state_conv)
    ret_s = jnp.zeros_like(state_ret)
    for layer in range(DEPTH):
        final_norm = layer == DEPTH - 1
        xp, conv_p, ret_p = _mixer_layer(layer, 0, final_norm, xp, rope_p, conv_p0, ret_p0,
                                         log_gamma, vecs, win, wsq, conv_p, ret_p)
        xs, conv_s, ret_s = _mixer_layer(layer, layer, final_norm, xs, rope_s, state_conv,
                                         state_ret, log_gamma, vecs, win, wsq, conv_s, ret_s)
    return (xp, xs, conv_p, ret_p, conv_s, ret_s)
```
